```python
import jax, jax.numpy as jnp
from jax import lax
import numpy as np

D_MODEL = 2048
BATCH = 1
SEQ = 16384
DEPTH = 2

CONV_WIDTH = D_MODEL // 2
CONV_K = 3
HEAD_DIM = 128
ATTN_WIDTH = D_MODEL // 2
N_HEADS = ATTN_WIDTH // HEAD_DIM
D_FF = 4 * D_MODEL
BLOCK_Q = 128
EPS = 1e-6

SPLIT_SIZES = (CONV_WIDTH, CONV_WIDTH, CONV_WIDTH,
               ATTN_WIDTH, ATTN_WIDTH, ATTN_WIDTH,
               N_HEADS, D_MODEL, D_MODEL)
IN_COLS = sum(SPLIT_SIZES)
SPLIT_POINTS = tuple(int(v) for v in np.cumsum(SPLIT_SIZES)[:-1])

kernel_name = "hybrid_conv_fox_gated_block"


def rmsnorm(x, g):
    xf = x.astype(jnp.float32)
    y = xf * lax.rsqrt(jnp.mean(xf * xf, axis=-1, keepdims=True) + EPS)
    return (y * g.astype(jnp.float32)).astype(x.dtype)


def causal_dwconv(u, w):
    s = u.shape[1]
    up = jnp.pad(u, ((0, 0), (CONV_K - 1, 0), (0, 0)))
    y = w[0] * up[:, 0:s, :]
    for j in range(1, CONV_K):
        y = y + w[j] * up[:, j:j + s, :]
    return y


def forgetting_attention(q, k, v, log_f):
    b, s, h, d = q.shape
    nb = s // BLOCK_Q
    c = jnp.cumsum(log_f, axis=1).transpose(0, 2, 1)
    qh = q.transpose(0, 2, 1, 3)
    kh = k.transpose(0, 2, 1, 3)
    vh = v.transpose(0, 2, 1, 3)
    scale = float(d) ** -0.5
    q_blocks = qh.reshape(b, h, nb, BLOCK_Q, d).transpose(2, 0, 1, 3, 4)
    c_blocks = c.reshape(b, h, nb, BLOCK_Q).transpose(2, 0, 1, 3)
    k_pos = jnp.arange(s)

    def one_block(args):
        qb, cb, i = args
        logits = jnp.einsum('bhqd,bhkd->bhqk', qb, kh).astype(jnp.float32) * scale
        logits = logits + cb[..., :, None] - c[..., None, :]
        q_pos = i * BLOCK_Q + jnp.arange(BLOCK_Q)
        causal = k_pos[None, :] <= q_pos[:, None]
        logits = jnp.where(causal, logits, -jnp.inf)
        p = jax.nn.softmax(logits, axis=-1)
        return jnp.einsum('bhqk,bhkd->bhqd', p.astype(vh.dtype), vh)

    out = lax.map(one_block, (q_blocks, c_blocks, jnp.arange(nb)))
    return out.transpose(1, 0, 3, 2, 4).reshape(b, s, h * d)


def setup_inputs(seed: int = 0) -> dict:
    key = jax.random.key(seed)
    ks = jax.random.split(key, 14)
    nrm = jax.random.normal
    x = nrm(ks[0], (BATCH, SEQ, D_MODEL), jnp.float32)
    g_mix = 1.0 + 0.02 * nrm(ks[1], (DEPTH, D_MODEL), jnp.float32)
    w_in = nrm(ks[2], (DEPTH, D_MODEL, IN_COLS), jnp.float32) * D_MODEL ** -0.5
    b_f = 2.0 + 0.5 * nrm(ks[3], (DEPTH, N_HEADS), jnp.float32)
    b_gate = 0.02 * nrm(ks[4], (DEPTH, 2 * D_MODEL), jnp.float32)
    conv_w = nrm(ks[5], (DEPTH, CONV_K, CONV_WIDTH), jnp.float32) * CONV_K ** -0.5
    w_conv_out = nrm(ks[6], (DEPTH, CONV_WIDTH, D_MODEL), jnp.float32) * CONV_WIDTH ** -0.5
    w_attn_out = nrm(ks[7], (DEPTH, ATTN_WIDTH, D_MODEL), jnp.float32) * ATTN_WIDTH ** -0.5
    w_mix_out = nrm(ks[8], (DEPTH, D_MODEL, D_MODEL), jnp.float32) * D_MODEL ** -0.5
    g_mlp = 1.0 + 0.02 * nrm(ks[9], (DEPTH, D_MODEL), jnp.float32)
    w_ff1 = nrm(ks[10], (DEPTH, D_MODEL, D_FF), jnp.float32) * D_MODEL ** -0.5
    w_ff2 = nrm(ks[11], (DEPTH, D_FF, D_MODEL), jnp.float32) * D_FF ** -0.5
    g_final = 1.0 + 0.02 * nrm(ks[12], (D_MODEL,), jnp.float32)
    return {"x": x, "g_mix": g_mix, "w_in": w_in, "b_f": b_f, "b_gate": b_gate,
            "conv_w": conv_w, "w_conv_out": w_conv_out, "w_attn_out": w_attn_out,
            "w_mix_out": w_mix_out, "g_mlp": g_mlp, "w_ff1": w_ff1, "w_ff2": w_ff2,
            "g_final": g_final}


def reference(x, g_mix, w_in, b_f, b_gate, conv_w, w_conv_out, w_attn_out,
              w_mix_out, g_mlp, w_ff1, w_ff2, g_final):
    b, s, _ = x.shape
    for l in range(DEPTH):
        h = rmsnorm(x, g_mix[l])
        z = jnp.einsum('bsd,dc->bsc', h, w_in[l])
        cb, cc, cv, q, k, v, f_logit, gate_c, gate_a = jnp.split(z, SPLIT_POINTS, axis=-1)

        conv_y = cb * causal_dwconv(cc * cv, conv_w[l])
        conv_branch = jnp.einsum('bsc,cd->bsd', conv_y, w_conv_out[l])

        log_f = jax.nn.log_sigmoid(f_logit.astype(jnp.float32) + b_f[l].astype(jnp.float32))
        attn = forgetting_attention(q.reshape(b, s, N_HEADS, HEAD_DIM),
                                    k.reshape(b, s, N_HEADS, HEAD_DIM),
                                    v.reshape(b, s, N_HEADS, HEAD_DIM), log_f)
        attn_branch = jnp.einsum('bsc,cd->bsd', attn, w_attn_out[l])

        gc = jax.nn.sigmoid(gate_c + b_gate[l, :D_MODEL])
        ga = jax.nn.sigmoid(gate_a + b_gate[l, D_MODEL:])
        merged = gc * conv_branch + ga * attn_branch
        x = x + jnp.einsum('bsd,de->bse', merged, w_mix_out[l])

        h = rmsnorm(x, g_mlp[l])
        u = jnp.square(jax.nn.relu(jnp.einsum('bsd,df->bsf', h, w_ff1[l])))
        x = x + jnp.einsum('bsf,fd->bsd', u, w_ff2[l])
    return rmsnorm(x, g_final)
```

```python
import functools

import jax
import jax.numpy as jnp
from jax import lax
from jax.experimental import pallas as pl
from jax.experimental.pallas import tpu as pltpu

EPS = 1e-6
HEAD_DIM = 128
CONV_K = 3

V7X_LANES = 128
V7X_MXU_DEPTH = 256
V7X_BF16_SUBLANES = 16
V7X_VMEM_BYTES = 64 * 1024 * 1024
MIB = 1024 * 1024

INPROJ_TM = 1024
INPROJ_TN = 1024
KV_BLOCK = 512
Q_BLOCK = 512
MIX_TM = 256
MLP_TM = 512
MLP_TF = 1024

MASK_VALUE = -1e30

BF16 = jnp.bfloat16
F32 = jnp.float32


def _rmsnorm(x, g):
    ms = jnp.mean(x * x, axis=-1, keepdims=True)
    return x * lax.rsqrt(ms + EPS) * g


def _split3_bf16(v):
    hi = v.astype(BF16).astype(F32)
    r1 = v - hi
    mid = r1.astype(BF16).astype(F32)
    lo = (r1 - mid).astype(BF16).astype(F32)
    return hi, mid, lo


def _params(semantics, vmem_bytes):
    assert vmem_bytes <= V7X_VMEM_BYTES - 4 * MIB, vmem_bytes
    return pltpu.CompilerParams(dimension_semantics=semantics, vmem_limit_bytes=int(vmem_bytes))


def _resident(block_shape, index_map):
    return pl.BlockSpec(block_shape, index_map, pipeline_mode=pl.Buffered(1))


def _inproj_kernel(x_ref, g_ref, w_ref, z_ref, h_ref):
    @pl.when(pl.program_id(1) == 0)
    def _():
        h_ref[...] = _rmsnorm(x_ref[...], g_ref[...]).astype(BF16)

    z_ref[...] = jnp.dot(h_ref[...], w_ref[...], preferred_element_type=F32).astype(z_ref.dtype)


def _inproj(x, g, w):
    s, d = x.shape
    n = w.shape[1]
    tm, tn = min(INPROJ_TM, s), INPROJ_TN
    assert s % tm == 0 and n % tn == 0
    vmem = (2 * tm * d * 4 + tm * d * 2 + 2 * d * tn * 2 + 2 * tm * tn * 2
            + tm * tn * 4 + tm * d * 4 + 4 * MIB)
    return pl.pallas_call(
        _inproj_kernel,
        grid=(s // tm, n // tn),
        in_specs=[pl.BlockSpec((tm, d), lambda i, j: (i, 0)),
                  pl.BlockSpec((1, d), lambda i, j: (0, 0)),
                  pl.BlockSpec((d, tn), lambda i, j: (0, j))],
        out_specs=pl.BlockSpec((tm, tn), lambda i, j: (i, j)),
        out_shape=jax.ShapeDtypeStruct((s, n), BF16),
        scratch_shapes=[pltpu.VMEM((tm, d), BF16)],
        compiler_params=_params(("arbitrary", "arbitrary"), vmem),
        name="inproj",
    )(x, g, w)


def _prep_kernel(x_ref, g_ref, wqt_ref, wk_ref, wvt_ref, wf_ref, bf_ref,
                 qt_ref, k_ref, vt_ref, r_ref, carry_ref, *, n_heads, scale):
    i = pl.program_id(0)
    tm = x_ref.shape[0]

    @pl.when(i == 0)
    def _():
        carry_ref[...] = jnp.zeros_like(carry_ref)

    h = _rmsnorm(x_ref[...], g_ref[...]).astype(BF16)
    nt = (((1,), (1,)), ((), ()))
    qt = lax.dot_general(wqt_ref[...], h, nt, preferred_element_type=F32) * scale
    vt = lax.dot_general(wvt_ref[...], h, nt, preferred_element_type=F32)
    k = jnp.dot(h, wk_ref[...], preferred_element_type=F32)
    f = jnp.dot(h, wf_ref[...], preferred_element_type=F32) + bf_ref[...]

    lane = lax.broadcasted_iota(jnp.int32, f.shape, 1)
    logf = jnp.where(lane < n_heads, jax.nn.log_sigmoid(f), 0.0)

    row = lax.broadcasted_iota(jnp.int32, (tm, tm), 0)
    col = lax.broadcasted_iota(jnp.int32, (tm, tm), 1)
    tril = (col <= row).astype(BF16)
    c_rel = None
    for part in _split3_bf16(logf):
        term = jnp.dot(tril, part.astype(BF16), preferred_element_type=F32)
        c_rel = term if c_rel is None else c_rel + term

    carry = carry_ref[...]
    r_ref[0] = carry
    carry_ref[...] = carry + c_rel[tm - 1:tm, :]

    nhi, nmid, nlo = _split3_bf16(-c_rel)
    bias_cols = (nhi + pltpu.roll(nmid, n_heads, 1) + pltpu.roll(nlo, 2 * n_heads, 1)).astype(BF16)

    sel_row = lax.broadcasted_iota(jnp.int32, (V7X_LANES, tm), 0)
    for hh in range(n_heads):
        sl = slice(hh * HEAD_DIM, (hh + 1) * HEAD_DIM)
        onehot = ((sel_row == hh) | (sel_row == n_heads + hh) | (sel_row == 2 * n_heads + hh))
        qt_ref[hh, 0:HEAD_DIM, :] = qt[sl, :].astype(BF16)
        qt_ref[hh, HEAD_DIM:, :] = onehot.astype(BF16)
        k_ref[hh, 0, :, 0:HEAD_DIM] = k[:, sl].astype(BF16)
        k_ref[hh, 0, :, HEAD_DIM:] = bias_cols
        vt_ref[hh, 0] = vt[sl, :].astype(BF16)


def _attn_prep(x, g, wqt, wk, wvt, wf, bf):
    s, d = x.shape
    a = wk.shape[1]
    n_heads = a // HEAD_DIM
    assert 3 * n_heads <= V7X_LANES and 2 * HEAD_DIM == V7X_MXU_DEPTH
    tm = KV_BLOCK
    assert s % tm == 0
    nb = s // tm
    vmem = (2 * tm * d * 4 + 3 * d * a * 2 + d * V7X_LANES * 2
            + 2 * n_heads * tm * (2 * V7X_MXU_DEPTH + HEAD_DIM) * 2
            + tm * d * 6 + 3 * tm * a * 4 + tm * tm * 2 + 6 * MIB)
    kern = functools.partial(_prep_kernel, n_heads=n_heads, scale=float(HEAD_DIM) ** -0.5)
    return pl.pallas_call(
        kern,
        grid=(nb,),
        in_specs=[pl.BlockSpec((tm, d), lambda i: (i, 0)),
                  _resident((1, d), lambda i: (0, 0)),
                  _resident((a, d), lambda i: (0, 0)),
                  _resident((d, a), lambda i: (0, 0)),
                  _resident((a, d), lambda i: (0, 0)),
                  _resident((d, V7X_LANES), lambda i: (0, 0)),
                  _resident((1, V7X_LANES), lambda i: (0, 0))],
        out_specs=[pl.BlockSpec((n_heads, V7X_MXU_DEPTH, tm), lambda i: (0, 0, i)),
                   pl.BlockSpec((n_heads, 1, tm, V7X_MXU_DEPTH), lambda i: (0, i, 0, 0)),
                   pl.BlockSpec((n_heads, 1, HEAD_DIM, tm), lambda i: (0, i, 0, 0)),
                   pl.BlockSpec((1, 8, V7X_LANES), lambda i: (i, 0, 0))],
        out_shape=[jax.ShapeDtypeStruct((n_heads, V7X_MXU_DEPTH, s), BF16),
                   jax.ShapeDtypeStruct((n_heads, nb, tm, V7X_MXU_DEPTH), BF16),
                   jax.ShapeDtypeStruct((n_heads, nb, HEAD_DIM, tm), BF16),
                   jax.ShapeDtypeStruct((nb, 8, V7X_LANES), F32)],
        scratch_shapes=[pltpu.VMEM((8, V7X_LANES), F32)],
        compiler_params=_params(("arbitrary",), vmem),
        name="attn_prep",
    )(x, g, wqt, wk, wvt, wf, bf)


def _flash_kernel(r_ref, qt_ref, k_ref, vt_ref, o_ref):
    hd = pl.program_id(0)
    qi = pl.program_id(1)
    tk = k_ref.shape[1]
    tq = qt_ref.shape[1]
    qt = qt_ref[...]

    def step(j, carry, masked):
        m, l, acc = carry
        s = jnp.dot(k_ref[j], qt, preferred_element_type=F32)
        if masked:
            key = j * tk + lax.broadcasted_iota(jnp.int32, (tk, tq), 0)
            qry = qi * tq + lax.broadcasted_iota(jnp.int32, (tk, tq), 1)
            s = jnp.where(key <= qry, s, MASK_VALUE)
        rb = r_ref[hd, j]
        m_new = jnp.maximum(m, jnp.max(s, axis=0, keepdims=True) - rb)
        alpha = jnp.exp(m - m_new)
        p = jnp.exp(s - (m_new + rb))
        l = alpha * l + jnp.sum(p, axis=0, keepdims=True)
        acc = alpha * acc + jnp.dot(vt_ref[j], p.astype(BF16), preferred_element_type=F32)
        return m_new, l, acc

    init = (jnp.full((1, tq), MASK_VALUE, F32), jnp.zeros((1, tq), F32),
            jnp.zeros((HEAD_DIM, tq), F32))
    n_full = (qi * tq) // tk
    carry = lax.fori_loop(0, n_full, functools.partial(step, masked=False), init)
    _, l, acc = step(n_full, carry, masked=True)
    o_ref[...] = (acc / l).T.astype(o_ref.dtype)


def _flash(r, qt, k, vt):
    n_heads, depth, s = qt.shape
    nb, tk = k.shape[1], k.shape[2]
    tq = Q_BLOCK
    assert tq == tk and s % tq == 0
    vmem = (2 * nb * tk * (depth + HEAD_DIM) * 2 + 2 * depth * tq * 2 + 2 * tq * HEAD_DIM * 2
            + 4 * tk * tq * 4 + 8 * MIB)
    return pl.pallas_call(
        _flash_kernel,
        grid=(n_heads, s // tq),
        in_specs=[pl.BlockSpec(memory_space=pltpu.SMEM),
                  pl.BlockSpec((None, depth, tq), lambda h, i: (h, 0, i)),
                  pl.BlockSpec((None, nb, tk, depth), lambda h, i: (h, 0, 0, 0)),
                  pl.BlockSpec((None, nb, HEAD_DIM, tk), lambda h, i: (h, 0, 0, 0))],
        out_specs=pl.BlockSpec((tq, HEAD_DIM), lambda h, i: (i, h)),
        out_shape=jax.ShapeDtypeStruct((s, n_heads * HEAD_DIM), BF16),
        compiler_params=_params(("arbitrary", "arbitrary"), vmem),
        name="flash",
    )(r, qt, k, vt)


def _mix_kernel(x_ref, gc_ref, ga_ref, cb_ref, cc_ref, cv_ref, ccp_ref, cvp_ref, at_ref,
                cw_ref, bg_ref, wco_ref, wao_ref, wmo_ref, o_ref, ext_ref):
    i = pl.program_id(0)
    tm = x_ref.shape[0]
    halo = ccp_ref.shape[0]
    d = x_ref.shape[1]

    prev = ccp_ref[...].astype(F32) * cvp_ref[...].astype(F32)
    ext_ref[0:halo, :] = jnp.where(i == 0, 0.0, prev)
    ext_ref[halo:, :] = cc_ref[...].astype(F32) * cv_ref[...].astype(F32)
    y = None
    for tap in range(CONV_K):
        shifted = ext_ref[pl.ds(halo - (CONV_K - 1) + tap, tm), :]
        term = cw_ref[tap:tap + 1, :] * shifted
        y = term if y is None else y + term
    conv_y = (cb_ref[...].astype(F32) * y).astype(BF16)

    conv_branch = jnp.dot(conv_y, wco_ref[...], preferred_element_type=F32)
    attn_branch = jnp.dot(at_ref[...], wao_ref[...], preferred_element_type=F32)
    gate_c = jax.nn.sigmoid(gc_ref[...].astype(F32) + bg_ref[:, 0:d])
    gate_a = jax.nn.sigmoid(ga_ref[...].astype(F32) + bg_ref[:, d:])
    merged = (gate_c * conv_branch + gate_a * attn_branch).astype(BF16)
    o_ref[...] = x_ref[...] + jnp.dot(merged, wmo_ref[...], preferred_element_type=F32)


def _mix(x, z, attn, conv_w, b_gate, wco, wao, wmo):
    s, d = x.shape
    c = wco.shape[0]
    a = wao.shape[0]
    assert d == 2 * c and z.shape[1] == 2 * d + 3 * c
    tm = MIX_TM
    halo = V7X_BF16_SUBLANES
    assert s % tm == 0 and tm % halo == 0 and halo >= CONV_K - 1
    rows_per_tile = tm // halo
    prev_rows = lambda i: jnp.maximum(i * rows_per_tile - 1, 0)
    vmem = (4 * tm * d * 4 + 4 * tm * d * 2 + 6 * tm * c * 2 + 2 * tm * a * 2
            + (c + a + d) * d * 2 + (tm + halo) * c * 4
            + 4 * tm * d * 4 + 3 * tm * c * 4 + 4 * MIB)
    return pl.pallas_call(
        _mix_kernel,
        grid=(s // tm,),
        in_specs=[pl.BlockSpec((tm, d), lambda i: (i, 0)),
                  pl.BlockSpec((tm, d), lambda i: (i, 0)),
                  pl.BlockSpec((tm, d), lambda i: (i, 1)),
                  pl.BlockSpec((tm, c), lambda i: (i, 4)),
                  pl.BlockSpec((tm, c), lambda i: (i, 5)),
                  pl.BlockSpec((tm, c), lambda i: (i, 6)),
                  pl.BlockSpec((halo, c), lambda i: (prev_rows(i), 5)),
                  pl.BlockSpec((halo, c), lambda i: (prev_rows(i), 6)),
                  pl.BlockSpec((tm, a), lambda i: (i, 0)),
                  _resident((CONV_K, c), lambda i: (0, 0)),
                  _resident((1, 2 * d), lambda i: (0, 0)),
                  _resident((c, d), lambda i: (0, 0)),
                  _resident((a, d), lambda i: (0, 0)),
                  _resident((d, d), lambda i: (0, 0))],
        out_specs=pl.BlockSpec((tm, d), lambda i: (i, 0)),
        out_shape=jax.ShapeDtypeStruct((s, d), F32),
        scratch_shapes=[pltpu.VMEM((tm + halo, c), F32)],
        compiler_params=_params(("arbitrary",), vmem),
        name="mix",
    )(x, z, z, z, z, z, z, z, attn, conv_w, b_gate, wco, wao, wmo)


def _mlp_kernel(x_ref, g_ref, w1_ref, w2_ref, gf_ref, o_ref, h_ref, *, final_norm):
    j = pl.program_id(1)

    @pl.when(j == 0)
    def _():
        x = x_ref[...]
        h_ref[...] = _rmsnorm(x, g_ref[...]).astype(BF16)
        o_ref[...] = x

    a = jnp.dot(h_ref[...], w1_ref[...], preferred_element_type=F32)
    u = jnp.square(jnp.maximum(a, 0.0)).astype(BF16)
    o_ref[...] += jnp.dot(u, w2_ref[...], preferred_element_type=F32)

    if final_norm:
        @pl.when(j == pl.num_programs(1) - 1)
        def _():
            o_ref[...] = _rmsnorm(o_ref[...], gf_ref[...])


def _mlp(x, g, w1, w2, g_final, final_norm):
    s, d = x.shape
    ff = w1.shape[1]
    tm, tf = MLP_TM, MLP_TF
    assert s % tm == 0 and ff % tf == 0
    vmem = (4 * tm * d * 4 + tm * d * 2 + 4 * d * tf * 2
            + tm * tf * 6 + 2 * tm * d * 4 + 4 * MIB)
    return pl.pallas_call(
        functools.partial(_mlp_kernel, final_norm=final_norm),
        grid=(s // tm, ff // tf),
        in_specs=[pl.BlockSpec((tm, d), lambda i, j: (i, 0)),
                  pl.BlockSpec((1, d), lambda i, j: (0, 0)),
                  pl.BlockSpec((d, tf), lambda i, j: (0, j)),
                  pl.BlockSpec((tf, d), lambda i, j: (j, 0)),
                  pl.BlockSpec((1, d), lambda i, j: (0, 0))],
        out_specs=pl.BlockSpec((tm, d), lambda i, j: (i, 0)),
        out_shape=jax.ShapeDtypeStruct((s, d), F32),
        scratch_shapes=[pltpu.VMEM((tm, d), BF16)],
        compiler_params=_params(("arbitrary", "arbitrary"), vmem),
        name="mlp",
    )(x, g, w1, w2, g_final)


def kernel(x, g_mix, w_in, b_f, b_gate, conv_w, w_conv_out, w_attn_out, w_mix_out, g_mlp, w_ff1, w_ff2, g_final):
    b, s, d = x.shape
    depth = g_mix.shape[0]
    c = w_conv_out.shape[1]
    a = w_attn_out.shape[1]
    n_heads = b_f.shape[1]
    assert a == n_heads * HEAD_DIM and w_in.shape[2] == 3 * c + 3 * a + n_heads + 2 * d
    o_q, o_k, o_v, o_f, o_g = 3 * c, 3 * c + a, 3 * c + 2 * a, 3 * c + 3 * a, 3 * c + 3 * a + n_heads

    outs = []
    for bi in range(b):
        xs = x[bi]
        for l in range(depth):
            w = w_in[l]
            w_cg = jnp.concatenate([w[:, o_g:], w[:, :o_q]], axis=1).astype(BF16)
            wqt = w[:, o_q:o_k].T.astype(BF16)
            wk = w[:, o_k:o_v].astype(BF16)
            wvt = w[:, o_v:o_f].T.astype(BF16)
            wf = jnp.pad(w[:, o_f:o_g], ((0, 0), (0, V7X_LANES - n_heads))).astype(BF16)
            bf = jnp.pad(b_f[l], (0, V7X_LANES - n_heads)).reshape(1, V7X_LANES)
            g1 = g_mix[l].reshape(1, d)

            z = _inproj(xs, g1, w_cg)
            qt, k, vt, r = _attn_prep(xs, g1, wqt, wk, wvt, wf, bf)
            r_heads = r[:, 0, :n_heads].T
            attn = _flash(r_heads, qt, k, vt)
            xs = _mix(xs, z, attn, conv_w[l], b_gate[l].reshape(1, 2 * d),
                      w_conv_out[l].astype(BF16), w_attn_out[l].astype(BF16), w_mix_out[l].astype(BF16))
            xs = _mlp(xs, g_mlp[l].reshape(1, d), w_ff1[l].astype(BF16), w_ff2[l].astype(BF16),
                      g_final.reshape(1, d), final_norm=(l == depth - 1))
        outs.append(xs)
    return jnp.stack(outs, axis=0)
```

```python
import functools

import jax
import jax.numpy as jnp
from jax import lax
from jax.experimental import pallas as pl
from jax.experimental.pallas import tpu as pltpu

EPS = 1e-6
HEAD_DIM = 128
CONV_K = 3

V7X_LANES = 128
V7X_MXU_DEPTH = 256
V7X_BF16_SUBLANES = 16
V7X_VMEM_BYTES = 64 * 1024 * 1024
MIB = 1024 * 1024

INPROJ_TM = 1024
INPROJ_TN = 1024
KV_BLOCK = 512
Q_BLOCK = 2 * KV_BLOCK
MIX_TM = 256
MLP_TM = 512
MLP_TF = 1024
SUM_ROWS = V7X_BF16_SUBLANES
V_ROWS = HEAD_DIM + SUM_ROWS

MASK_VALUE = -1e30
LOG2_E = 1.4426950408889634

BF16 = jnp.bfloat16
F32 = jnp.float32


def _rmsnorm(x, g):
    ms = jnp.mean(x * x, axis=-1, keepdims=True)
    return x * lax.rsqrt(ms + EPS) * g


def _split3_bf16(v):
    hi = v.astype(BF16).astype(F32)
    r1 = v - hi
    mid = r1.astype(BF16).astype(F32)
    lo = (r1 - mid).astype(BF16).astype(F32)
    return hi, mid, lo


def _params(semantics, vmem_bytes, flags=None):
    assert vmem_bytes <= V7X_VMEM_BYTES - 4 * MIB, vmem_bytes
    return pltpu.CompilerParams(dimension_semantics=semantics, vmem_limit_bytes=int(vmem_bytes), flags=flags)


def _resident(block_shape, index_map):
    return pl.BlockSpec(block_shape, index_map, pipeline_mode=pl.Buffered(1))


def _inproj_kernel(x_ref, g_ref, w_ref, z_ref, h_ref):
    @pl.when(pl.program_id(1) == 0)
    def _():
        h_ref[...] = _rmsnorm(x_ref[...], g_ref[...]).astype(BF16)

    z_ref[...] = jnp.dot(h_ref[...], w_ref[...], preferred_element_type=F32).astype(z_ref.dtype)


def _inproj(x, g, w):
    s, d = x.shape
    n = w.shape[1]
    tm, tn = min(INPROJ_TM, s), INPROJ_TN
    assert s % tm == 0 and n % tn == 0
    vmem = (2 * tm * d * 4 + tm * d * 2 + 2 * d * tn * 2 + 2 * tm * tn * 2
            + tm * tn * 4 + tm * d * 4 + 4 * MIB)
    return pl.pallas_call(
        _inproj_kernel,
        grid=(s // tm, n // tn),
        in_specs=[pl.BlockSpec((tm, d), lambda i, j: (i, 0)),
                  pl.BlockSpec((1, d), lambda i, j: (0, 0)),
                  pl.BlockSpec((d, tn), lambda i, j: (0, j))],
        out_specs=pl.BlockSpec((tm, tn), lambda i, j: (i, j)),
        out_shape=jax.ShapeDtypeStruct((s, n), BF16),
        scratch_shapes=[pltpu.VMEM((tm, d), BF16)],
        compiler_params=_params(("arbitrary", "arbitrary"), vmem),
        name="inproj",
    )(x, g, w)


def _prep_kernel(x_ref, g_ref, wqt_ref, wk_ref, wvt_ref, wf_ref, bf_ref,
                 qt_ref, k_ref, vt_ref, r_ref, carry_ref, *, n_heads, scale):
    i = pl.program_id(0)
    tm = x_ref.shape[0]

    @pl.when(i == 0)
    def _():
        carry_ref[...] = jnp.zeros_like(carry_ref)

    h = _rmsnorm(x_ref[...], g_ref[...]).astype(BF16)
    nt = (((1,), (1,)), ((), ()))
    qt = lax.dot_general(wqt_ref[...], h, nt, preferred_element_type=F32) * scale
    vt = lax.dot_general(wvt_ref[...], h, nt, preferred_element_type=F32)
    k = jnp.dot(h, wk_ref[...], preferred_element_type=F32)
    f = jnp.dot(h, wf_ref[...], preferred_element_type=F32) + bf_ref[...]

    lane = lax.broadcasted_iota(jnp.int32, f.shape, 1)
    logf = jnp.where(lane < n_heads, jax.nn.log_sigmoid(f) * LOG2_E, 0.0)

    row = lax.broadcasted_iota(jnp.int32, (tm, tm), 0)
    col = lax.broadcasted_iota(jnp.int32, (tm, tm), 1)
    tril = (col <= row).astype(BF16)
    c_rel = None
    for part in _split3_bf16(logf):
        term = jnp.dot(tril, part.astype(BF16), preferred_element_type=F32)
        c_rel = term if c_rel is None else c_rel + term

    carry = carry_ref[...]
    r_ref[0] = carry
    carry_ref[...] = carry + c_rel[tm - 1:tm, :]

    nhi, nmid, nlo = _split3_bf16(-c_rel)
    bias_cols = (nhi + pltpu.roll(nmid, n_heads, 1) + pltpu.roll(nlo, 2 * n_heads, 1)).astype(BF16)

    sel_row = lax.broadcasted_iota(jnp.int32, (V7X_LANES, tm), 0)
    sum_rows = (lax.broadcasted_iota(jnp.int32, (SUM_ROWS, tm), 0) == 0).astype(BF16)
    for hh in range(n_heads):
        sl = slice(hh * HEAD_DIM, (hh + 1) * HEAD_DIM)
        onehot = ((sel_row == hh) | (sel_row == n_heads + hh) | (sel_row == 2 * n_heads + hh))
        qt_ref[hh, 0:HEAD_DIM, :] = qt[sl, :].astype(BF16)
        qt_ref[hh, HEAD_DIM:, :] = onehot.astype(BF16)
        k_ref[hh, 0, :, 0:HEAD_DIM] = k[:, sl].astype(BF16)
        k_ref[hh, 0, :, HEAD_DIM:] = bias_cols
        vt_ref[hh, 0, 0:HEAD_DIM, :] = vt[sl, :].astype(BF16)
        vt_ref[hh, 0, HEAD_DIM:, :] = sum_rows


def _attn_prep(x, g, wqt, wk, wvt, wf, bf):
    s, d = x.shape
    a = wk.shape[1]
    n_heads = a // HEAD_DIM
    assert 3 * n_heads <= V7X_LANES and 2 * HEAD_DIM == V7X_MXU_DEPTH
    tm = KV_BLOCK
    assert s % tm == 0
    nb = s // tm
    vmem = (2 * tm * d * 4 + 3 * d * a * 2 + d * V7X_LANES * 2
            + 2 * n_heads * tm * (2 * V7X_MXU_DEPTH + V_ROWS) * 2
            + tm * d * 6 + 3 * tm * a * 4 + tm * tm * 2 + 6 * MIB)
    kern = functools.partial(_prep_kernel, n_heads=n_heads, scale=LOG2_E * float(HEAD_DIM) ** -0.5)
    return pl.pallas_call(
        kern,
        grid=(nb,),
        in_specs=[pl.BlockSpec((tm, d), lambda i: (i, 0)),
                  _resident((1, d), lambda i: (0, 0)),
                  _resident((a, d), lambda i: (0, 0)),
                  _resident((d, a), lambda i: (0, 0)),
                  _resident((a, d), lambda i: (0, 0)),
                  _resident((d, V7X_LANES), lambda i: (0, 0)),
                  _resident((1, V7X_LANES), lambda i: (0, 0))],
        out_specs=[pl.BlockSpec((n_heads, V7X_MXU_DEPTH, tm), lambda i: (0, 0, i)),
                   pl.BlockSpec((n_heads, 1, tm, V7X_MXU_DEPTH), lambda i: (0, i, 0, 0)),
                   pl.BlockSpec((n_heads, 1, V_ROWS, tm), lambda i: (0, i, 0, 0)),
                   pl.BlockSpec((1, 8, V7X_LANES), lambda i: (i, 0, 0))],
        out_shape=[jax.ShapeDtypeStruct((n_heads, V7X_MXU_DEPTH, s), BF16),
                   jax.ShapeDtypeStruct((n_heads, nb, tm, V7X_MXU_DEPTH), BF16),
                   jax.ShapeDtypeStruct((n_heads, nb, V_ROWS, tm), BF16),
                   jax.ShapeDtypeStruct((nb, 8, V7X_LANES), F32)],
        scratch_shapes=[pltpu.VMEM((8, V7X_LANES), F32)],
        compiler_params=_params(("arbitrary",), vmem),
        name="attn_prep",
    )(x, g, wqt, wk, wvt, wf, bf)


def _flash_kernel(r_ref, qt_ref, k_ref, vt_ref, o_ref, s_ref, p_ref, acc_ref, st_ref):
    hd = pl.program_id(0)
    qi = pl.program_id(1)
    tk = k_ref.shape[1]
    tq = qt_ref.shape[1]

    m_at, alpha_at, bm_at = 0, 1, 3

    def scores(blk, slot):
        s = jnp.dot(k_ref[blk], qt_ref[...], preferred_element_type=F32)
        s_ref[slot] = s
        st_ref[bm_at + slot] = jnp.max(s, axis=0, keepdims=True)

    def softmax(blk, slot, mask=None):
        s = s_ref[slot]
        if mask is None:
            bm = st_ref[bm_at + slot]
        else:
            s = jnp.where(mask, s, MASK_VALUE)
            bm = jnp.max(s, axis=0, keepdims=True)
        rb = r_ref[hd, blk]
        m = st_ref[m_at]
        m_new = jnp.maximum(m, bm - rb)
        alpha = jnp.exp2(m - m_new)
        p = jnp.exp2(s - (m_new + rb))
        p_ref[slot] = p.astype(BF16)
        st_ref[m_at] = m_new
        st_ref[alpha_at + slot] = alpha

    def pv(blk, slot):
        acc_ref[...] = st_ref[alpha_at + slot] * acc_ref[...] + jnp.dot(
            vt_ref[blk], p_ref[slot], preferred_element_type=F32)

    def step(t, slot, mask=None, last=False):
        if not last:
            scores(t + 1, 1 - slot)
        softmax(t, slot, mask)
        pv(jnp.maximum(t - 1, 0), 1 - slot)

    acc_ref[...] = jnp.zeros_like(acc_ref)
    p_ref[1] = jnp.zeros(p_ref.shape[1:], p_ref.dtype)
    st_ref[alpha_at + 1] = jnp.ones((1, tq), F32)
    st_ref[m_at] = jnp.full((1, tq), MASK_VALUE, F32)
    scores(0, 0)

    def pair(w, carry):
        step(2 * w, 0)
        step(2 * w + 1, 1)
        return carry

    lax.fori_loop(0, qi, pair, 0)

    row = lax.broadcasted_iota(jnp.int32, (tk, tq), 0)
    col = lax.broadcasted_iota(jnp.int32, (tk, tq), 1)
    d0 = 2 * qi
    step(d0, 0, mask=row <= col)
    step(d0 + 1, 1, mask=row + tk <= col, last=True)
    pv(d0 + 1, 1)
    o_ref[...] = (acc_ref[0:HEAD_DIM, :] / acc_ref[HEAD_DIM:HEAD_DIM + 1, :]).T.astype(o_ref.dtype)


def _flash(r, qt, k, vt):
    n_heads, depth, s = qt.shape
    nb, tk = k.shape[1], k.shape[2]
    tq = Q_BLOCK
    assert tq == 2 * tk and s % tq == 0
    v_rows = vt.shape[2]
    vmem = (2 * nb * tk * (depth + v_rows) * 2 + 2 * depth * tq * 2 + 2 * tq * HEAD_DIM * 2
            + 2 * tk * tq * 6 + v_rows * tq * 4 + 3 * tk * tq * 4 + 4 * MIB)
    return pl.pallas_call(
        _flash_kernel,
        grid=(n_heads, s // tq),
        in_specs=[pl.BlockSpec(memory_space=pltpu.SMEM),
                  pl.BlockSpec((None, depth, tq), lambda h, i: (h, 0, i)),
                  pl.BlockSpec((None, nb, tk, depth), lambda h, i: (h, 0, 0, 0)),
                  pl.BlockSpec((None, nb, v_rows, tk), lambda h, i: (h, 0, 0, 0))],
        out_specs=pl.BlockSpec((tq, HEAD_DIM), lambda h, i: (i, h)),
        out_shape=jax.ShapeDtypeStruct((s, n_heads * HEAD_DIM), BF16),
        scratch_shapes=[pltpu.VMEM((2, tk, tq), F32),
                        pltpu.VMEM((2, tk, tq), BF16),
                        pltpu.VMEM((v_rows, tq), F32),
                        pltpu.VMEM((5, 1, tq), F32)],
        compiler_params=_params(("arbitrary", "arbitrary"), vmem),
        name="flash",
    )(r, qt, k, vt)


def _mix_kernel(x_ref, gc_ref, ga_ref, cb_ref, cc_ref, cv_ref, ccp_ref, cvp_ref, at_ref,
                cw_ref, bg_ref, wco_ref, wao_ref, wmo_ref, o_ref, ext_ref):
    i = pl.program_id(0)
    tm = x_ref.shape[0]
    halo = ccp_ref.shape[0]
    d = x_ref.shape[1]

    prev = ccp_ref[...].astype(F32) * cvp_ref[...].astype(F32)
    ext_ref[0:halo, :] = jnp.where(i == 0, 0.0, prev)
    ext_ref[halo:, :] = cc_ref[...].astype(F32) * cv_ref[...].astype(F32)
    y = None
    for tap in range(CONV_K):
        shifted = ext_ref[pl.ds(halo - (CONV_K - 1) + tap, tm), :]
        term = cw_ref[tap:tap + 1, :] * shifted
        y = term if y is None else y + term
    conv_y = (cb_ref[...].astype(F32) * y).astype(BF16)

    conv_branch = jnp.dot(conv_y, wco_ref[...], preferred_element_type=F32)
    attn_branch = jnp.dot(at_ref[...], wao_ref[...], preferred_element_type=F32)
    gate_c = jax.nn.sigmoid(gc_ref[...].astype(F32) + bg_ref[:, 0:d])
    gate_a = jax.nn.sigmoid(ga_ref[...].astype(F32) + bg_ref[:, d:])
    merged = (gate_c * conv_branch + gate_a * attn_branch).astype(BF16)
    o_ref[...] = x_ref[...] + jnp.dot(merged, wmo_ref[...], preferred_element_type=F32)


def _mix(x, z, attn, conv_w, b_gate, wco, wao, wmo):
    s, d = x.shape
    c = wco.shape[0]
    a = wao.shape[0]
    assert d == 2 * c and z.shape[1] == 2 * d + 3 * c
    tm = MIX_TM
    halo = V7X_BF16_SUBLANES
    assert s % tm == 0 and tm % halo == 0 and halo >= CONV_K - 1
    rows_per_tile = tm // halo
    prev_rows = lambda i: jnp.maximum(i * rows_per_tile - 1, 0)
    vmem = (4 * tm * d * 4 + 4 * tm * d * 2 + 6 * tm * c * 2 + 2 * tm * a * 2
            + (c + a + d) * d * 2 + (tm + halo) * c * 4
            + 4 * tm * d * 4 + 3 * tm * c * 4 + 4 * MIB)
    return pl.pallas_call(
        _mix_kernel,
        grid=(s // tm,),
        in_specs=[pl.BlockSpec((tm, d), lambda i: (i, 0)),
                  pl.BlockSpec((tm, d), lambda i: (i, 0)),
                  pl.BlockSpec((tm, d), lambda i: (i, 1)),
                  pl.BlockSpec((tm, c), lambda i: (i, 4)),
                  pl.BlockSpec((tm, c), lambda i: (i, 5)),
                  pl.BlockSpec((tm, c), lambda i: (i, 6)),
                  pl.BlockSpec((halo, c), lambda i: (prev_rows(i), 5)),
                  pl.BlockSpec((halo, c), lambda i: (prev_rows(i), 6)),
                  pl.BlockSpec((tm, a), lambda i: (i, 0)),
                  _resident((CONV_K, c), lambda i: (0, 0)),
                  _resident((1, 2 * d), lambda i: (0, 0)),
                  _resident((c, d), lambda i: (0, 0)),
                  _resident((a, d), lambda i: (0, 0)),
                  _resident((d, d), lambda i: (0, 0))],
        out_specs=pl.BlockSpec((tm, d), lambda i: (i, 0)),
        out_shape=jax.ShapeDtypeStruct((s, d), F32),
        scratch_shapes=[pltpu.VMEM((tm + halo, c), F32)],
        compiler_params=_params(("arbitrary",), vmem),
        name="mix",
    )(x, z, z, z, z, z, z, z, attn, conv_w, b_gate, wco, wao, wmo)


def _mlp_kernel(x_ref, g_ref, w1_ref, w2_ref, gf_ref, o_ref, h_ref, *, final_norm):
    j = pl.program_id(1)

    @pl.when(j == 0)
    def _():
        x = x_ref[...]
        h_ref[...] = _rmsnorm(x, g_ref[...]).astype(BF16)
        o_ref[...] = x

    a = jnp.dot(h_ref[...], w1_ref[...], preferred_element_type=F32)
    u = jnp.square(jnp.maximum(a, 0.0)).astype(BF16)
    o_ref[...] += jnp.dot(u, w2_ref[...], preferred_element_type=F32)

    if final_norm:
        @pl.when(j == pl.num_programs(1) - 1)
        def _():
            o_ref[...] = _rmsnorm(o_ref[...], gf_ref[...])


def _mlp(x, g, w1, w2, g_final, final_norm):
    s, d = x.shape
    ff = w1.shape[1]
    tm, tf = MLP_TM, MLP_TF
    assert s % tm == 0 and ff % tf == 0
    vmem = (4 * tm * d * 4 + tm * d * 2 + 4 * d * tf * 2
            + tm * tf * 6 + 2 * tm * d * 4 + 4 * MIB)
    return pl.pallas_call(
        functools.partial(_mlp_kernel, final_norm=final_norm),
        grid=(s // tm, ff // tf),
        in_specs=[pl.BlockSpec((tm, d), lambda i, j: (i, 0)),
                  pl.BlockSpec((1, d), lambda i, j: (0, 0)),
                  pl.BlockSpec((d, tf), lambda i, j: (0, j)),
                  pl.BlockSpec((tf, d), lambda i, j: (j, 0)),
                  pl.BlockSpec((1, d), lambda i, j: (0, 0))],
        out_specs=pl.BlockSpec((tm, d), lambda i, j: (i, 0)),
        out_shape=jax.ShapeDtypeStruct((s, d), F32),
        scratch_shapes=[pltpu.VMEM((tm, d), BF16)],
        compiler_params=_params(("arbitrary", "arbitrary"), vmem),
        name="mlp",
    )(x, g, w1, w2, g_final)


def kernel(x, g_mix, w_in, b_f, b_gate, conv_w, w_conv_out, w_attn_out, w_mix_out, g_mlp, w_ff1, w_ff2, g_final):
    b, s, d = x.shape
    depth = g_mix.shape[0]
    c = w_conv_out.shape[1]
    a = w_attn_out.shape[1]
    n_heads = b_f.shape[1]
    assert a == n_heads * HEAD_DIM and w_in.shape[2] == 3 * c + 3 * a + n_heads + 2 * d
    o_q, o_k, o_v, o_f, o_g = 3 * c, 3 * c + a, 3 * c + 2 * a, 3 * c + 3 * a, 3 * c + 3 * a + n_heads

    outs = []
    for bi in range(b):
        xs = x[bi]
        for l in range(depth):
            w = w_in[l]
            w_cg = jnp.concatenate([w[:, o_g:], w[:, :o_q]], axis=1).astype(BF16)
            wqt = w[:, o_q:o_k].T.astype(BF16)
            wk = w[:, o_k:o_v].astype(BF16)
            wvt = w[:, o_v:o_f].T.astype(BF16)
            wf = jnp.pad(w[:, o_f:o_g], ((0, 0), (0, V7X_LANES - n_heads))).astype(BF16)
            bf = jnp.pad(b_f[l], (0, V7X_LANES - n_heads)).reshape(1, V7X_LANES)
            g1 = g_mix[l].reshape(1, d)

            z = _inproj(xs, g1, w_cg)
            qt, k, vt, r = _attn_prep(xs, g1, wqt, wk, wvt, wf, bf)
            r_heads = r[:, 0, :n_heads].T
            attn = _flash(r_heads, qt, k, vt)
            xs = _mix(xs, z, attn, conv_w[l], b_gate[l].reshape(1, 2 * d),
                      w_conv_out[l].astype(BF16), w_attn_out[l].astype(BF16), w_mix_out[l].astype(BF16))
            xs = _mlp(xs, g_mlp[l].reshape(1, d), w_ff1[l].astype(BF16), w_ff2[l].astype(BF16),
                      g_final.reshape(1, d), final_norm=(l == depth - 1))
        outs.append(xs)
    return jnp.stack(outs, axis=0)
```

```python
import functools

import jax
import jax.numpy as jnp
from jax import lax
from jax.experimental import pallas as pl
from jax.experimental.pallas import tpu as pltpu

EPS = 1e-6
HEAD_DIM = 128
CONV_K = 3

V7X_LANES = 128
V7X_MXU_DEPTH = 256
V7X_F32_SUBLANES = 8
V7X_BF16_SUBLANES = 16
V7X_VMEM_BYTES = 64 * 1024 * 1024
MIB = 1024 * 1024

INPROJ_TM = 1024
INPROJ_TN = 1024
KV_BLOCK = 512
Q_BLOCK = 2 * KV_BLOCK
MIX_TM = 256
MLP_TM = 512
MLP_TF = 1024
SUM_ROWS = V7X_BF16_SUBLANES
V_ROWS = HEAD_DIM + SUM_ROWS

MASK_VALUE = -1e30
LOG2_E = 1.4426950408889634

BF16 = jnp.bfloat16
F32 = jnp.float32


def _rmsnorm(x, g):
    ms = jnp.mean(x * x, axis=-1, keepdims=True)
    return x * lax.rsqrt(ms + EPS) * g


def _split3_bf16(v):
    hi = v.astype(BF16).astype(F32)
    r1 = v - hi
    mid = r1.astype(BF16).astype(F32)
    lo = (r1 - mid).astype(BF16).astype(F32)
    return hi, mid, lo


def _params(semantics, vmem_bytes, flags=None):
    assert vmem_bytes <= V7X_VMEM_BYTES - 4 * MIB, vmem_bytes
    return pltpu.CompilerParams(dimension_semantics=semantics, vmem_limit_bytes=int(vmem_bytes), flags=flags)


def _resident(block_shape, index_map):
    return pl.BlockSpec(block_shape, index_map, pipeline_mode=pl.Buffered(1))


def _inproj_kernel(x_ref, g_ref, w_ref, z_ref, h_ref):
    @pl.when(pl.program_id(1) == 0)
    def _():
        h_ref[...] = _rmsnorm(x_ref[...], g_ref[...]).astype(BF16)

    z_ref[...] = jnp.dot(h_ref[...], w_ref[...], preferred_element_type=F32).astype(z_ref.dtype)


def _inproj(x, g, w):
    s, d = x.shape
    n = w.shape[1]
    tm, tn = min(INPROJ_TM, s), INPROJ_TN
    assert s % tm == 0 and n % tn == 0
    vmem = (2 * tm * d * 4 + tm * d * 2 + 2 * d * tn * 2 + 2 * tm * tn * 2
            + tm * tn * 4 + tm * d * 4 + 4 * MIB)
    return pl.pallas_call(
        _inproj_kernel,
        grid=(s // tm, n // tn),
        in_specs=[pl.BlockSpec((tm, d), lambda i, j: (i, 0)),
                  pl.BlockSpec((1, d), lambda i, j: (0, 0)),
                  pl.BlockSpec((d, tn), lambda i, j: (0, j))],
        out_specs=pl.BlockSpec((tm, tn), lambda i, j: (i, j)),
        out_shape=jax.ShapeDtypeStruct((s, n), BF16),
        scratch_shapes=[pltpu.VMEM((tm, d), BF16)],
        compiler_params=_params(("arbitrary", "arbitrary"), vmem),
        name="inproj",
    )(x, g, w)


def _prep_kernel(x_ref, g_ref, wqt_ref, wk_ref, wvt_ref, wf_ref, bf_ref,
                 qt_ref, k_ref, vt_ref, r_ref, carry_ref, *, n_heads, scale):
    i = pl.program_id(0)
    tm = x_ref.shape[0]

    @pl.when(i == 0)
    def _():
        carry_ref[...] = jnp.zeros_like(carry_ref)

    h = _rmsnorm(x_ref[...], g_ref[...]).astype(BF16)
    nt = (((1,), (1,)), ((), ()))
    qt = lax.dot_general(wqt_ref[...], h, nt, preferred_element_type=F32) * scale
    vt = lax.dot_general(wvt_ref[...], h, nt, preferred_element_type=F32)
    k = jnp.dot(h, wk_ref[...], preferred_element_type=F32)
    f = jnp.dot(h, wf_ref[...], preferred_element_type=F32) + bf_ref[...]

    lane = lax.broadcasted_iota(jnp.int32, f.shape, 1)
    logf = jnp.where(lane < n_heads, jax.nn.log_sigmoid(f) * LOG2_E, 0.0)

    row = lax.broadcasted_iota(jnp.int32, (tm, tm), 0)
    col = lax.broadcasted_iota(jnp.int32, (tm, tm), 1)
    tril = (col <= row).astype(BF16)
    c_rel = None
    for part in _split3_bf16(logf):
        term = jnp.dot(tril, part.astype(BF16), preferred_element_type=F32)
        c_rel = term if c_rel is None else c_rel + term

    carry = carry_ref[...]
    r_ref[0] = carry
    carry_ref[...] = carry + c_rel[tm - 1:tm, :]

    nhi, nmid, nlo = _split3_bf16(-c_rel)
    bias_cols = (nhi + pltpu.roll(nmid, n_heads, 1) + pltpu.roll(nlo, 2 * n_heads, 1)).astype(BF16)

    sel_row = lax.broadcasted_iota(jnp.int32, (V7X_LANES, tm), 0)
    sum_rows = (lax.broadcasted_iota(jnp.int32, (SUM_ROWS, tm), 0) == 0).astype(BF16)
    for hh in range(n_heads):
        sl = slice(hh * HEAD_DIM, (hh + 1) * HEAD_DIM)
        onehot = ((sel_row == hh) | (sel_row == n_heads + hh) | (sel_row == 2 * n_heads + hh))
        qt_ref[hh, 0:HEAD_DIM, :] = qt[sl, :].astype(BF16)
        qt_ref[hh, HEAD_DIM:, :] = onehot.astype(BF16)
        k_ref[hh, 0, :, 0:HEAD_DIM] = k[:, sl].astype(BF16)
        k_ref[hh, 0, :, HEAD_DIM:] = bias_cols
        vt_ref[hh, 0, 0:HEAD_DIM, :] = vt[sl, :].astype(BF16)
        vt_ref[hh, 0, HEAD_DIM:, :] = sum_rows


def _attn_prep(x, g, wqt, wk, wvt, wf, bf):
    s, d = x.shape
    a = wk.shape[1]
    n_heads = a // HEAD_DIM
    assert 3 * n_heads <= V7X_LANES and 2 * HEAD_DIM == V7X_MXU_DEPTH
    tm = KV_BLOCK
    assert s % tm == 0
    nb = s // tm
    vmem = (2 * tm * d * 4 + 3 * d * a * 2 + d * V7X_LANES * 2
            + 2 * n_heads * tm * (2 * V7X_MXU_DEPTH + V_ROWS) * 2
            + tm * d * 6 + 3 * tm * a * 4 + tm * tm * 2 + 6 * MIB)
    kern = functools.partial(_prep_kernel, n_heads=n_heads, scale=LOG2_E * float(HEAD_DIM) ** -0.5)
    return pl.pallas_call(
        kern,
        grid=(nb,),
        in_specs=[pl.BlockSpec((tm, d), lambda i: (i, 0)),
                  _resident((1, d), lambda i: (0, 0)),
                  _resident((a, d), lambda i: (0, 0)),
                  _resident((d, a), lambda i: (0, 0)),
                  _resident((a, d), lambda i: (0, 0)),
                  _resident((d, V7X_LANES), lambda i: (0, 0)),
                  _resident((1, V7X_LANES), lambda i: (0, 0))],
        out_specs=[pl.BlockSpec((n_heads, V7X_MXU_DEPTH, tm), lambda i: (0, 0, i)),
                   pl.BlockSpec((n_heads, 1, tm, V7X_MXU_DEPTH), lambda i: (0, i, 0, 0)),
                   pl.BlockSpec((n_heads, 1, V_ROWS, tm), lambda i: (0, i, 0, 0)),
                   pl.BlockSpec((1, V7X_F32_SUBLANES, V7X_LANES), lambda i: (i, 0, 0))],
        out_shape=[jax.ShapeDtypeStruct((n_heads, V7X_MXU_DEPTH, s), BF16),
                   jax.ShapeDtypeStruct((n_heads, nb, tm, V7X_MXU_DEPTH), BF16),
                   jax.ShapeDtypeStruct((n_heads, nb, V_ROWS, tm), BF16),
                   jax.ShapeDtypeStruct((nb, V7X_F32_SUBLANES, V7X_LANES), F32)],
        scratch_shapes=[pltpu.VMEM((V7X_F32_SUBLANES, V7X_LANES), F32)],
        compiler_params=_params(("arbitrary",), vmem),
        name="attn_prep",
    )(x, g, wqt, wk, wvt, wf, bf)


def _flash_kernel(r_ref, qt_ref, k_ref, vt_ref, o_ref, s_ref, acc_ref):
    hd = pl.program_id(0)
    qi = pl.program_id(1)
    tk = k_ref.shape[1]
    tq = qt_ref.shape[1]

    sw = V7X_MXU_DEPTH
    strips = [slice(c * sw, (c + 1) * sw) for c in range(tq // sw)]

    def scores(blk, c):
        s = jnp.dot(k_ref[blk], qt_ref[:, strips[c]], preferred_element_type=F32)
        s_ref[:, strips[c]] = s
        return jnp.max(s, axis=0, keepdims=True)

    def attend(blk, c, m, bm, key0=None):
        s = s_ref[:, strips[c]]
        if key0 is not None:
            key = key0 + lax.broadcasted_iota(jnp.int32, (tk, sw), 0)
            qry = c * sw + lax.broadcasted_iota(jnp.int32, (tk, sw), 1)
            s = jnp.where(key <= qry, s, MASK_VALUE)
            bm = jnp.max(s, axis=0, keepdims=True)
        rb = r_ref[hd, blk]
        m_new = jnp.maximum(m, bm - rb)
        alpha = jnp.exp2(m - m_new)
        p = jnp.exp2(s - (m_new + rb)).astype(BF16)
        acc_ref[:, strips[c]] = alpha * acc_ref[:, strips[c]] + jnp.dot(
            vt_ref[blk], p, preferred_element_type=F32)
        return m_new

    def visible(key0, c):
        if key0 is None or key0 + tk - 1 <= c * sw:
            return "all"
        return "none" if key0 > (c + 1) * sw - 1 else "some"

    def block(t, carry, key0=None, next_key0=None, last=False):
        out = []
        for c in range(len(strips)):
            m, bm = carry[c]
            if visible(key0, c) != "none":
                m = attend(t, c, m, bm, key0 if visible(key0, c) == "some" else None)
            if not last and visible(next_key0, c) != "none":
                bm = scores(t + 1, c)
            out.append((m, bm))
        return tuple(out)

    acc_ref[...] = jnp.zeros_like(acc_ref)
    init = tuple((jnp.full((1, sw), MASK_VALUE, F32), scores(0, c)) for c in range(len(strips)))
    carry = lax.fori_loop(0, qi, lambda w, carry: block(2 * w + 1, block(2 * w, carry)), init)

    carry = block(2 * qi, carry, key0=0, next_key0=tk)
    block(2 * qi + 1, carry, key0=tk, last=True)
    o_ref[...] = (acc_ref[0:HEAD_DIM, :] / acc_ref[HEAD_DIM:HEAD_DIM + 1, :]).T.astype(o_ref.dtype)


def _flash(r, qt, k, vt):
    n_heads, depth, s = qt.shape
    nb, tk = k.shape[1], k.shape[2]
    tq = Q_BLOCK
    assert tq == 2 * tk and s % tq == 0
    v_rows = vt.shape[2]
    vmem = (2 * nb * tk * (depth + v_rows) * 2 + 2 * depth * tq * 2 + 2 * tq * HEAD_DIM * 2
            + 2 * tk * tq * 6 + v_rows * tq * 4 + 3 * tk * tq * 4 + 4 * MIB)
    return pl.pallas_call(
        _flash_kernel,
        grid=(n_heads, s // tq),
        in_specs=[pl.BlockSpec(memory_space=pltpu.SMEM),
                  pl.BlockSpec((None, depth, tq), lambda h, i: (h, 0, i)),
                  pl.BlockSpec((None, nb, tk, depth), lambda h, i: (h, 0, 0, 0)),
                  pl.BlockSpec((None, nb, v_rows, tk), lambda h, i: (h, 0, 0, 0))],
        out_specs=pl.BlockSpec((tq, HEAD_DIM), lambda h, i: (i, h)),
        out_shape=jax.ShapeDtypeStruct((s, n_heads * HEAD_DIM), BF16),
        scratch_shapes=[pltpu.VMEM((tk, tq), F32),
                        pltpu.VMEM((v_rows, tq), F32)],
        compiler_params=_params(("arbitrary", "arbitrary"), vmem),
        name="flash",
    )(r, qt, k, vt)


def _mix_kernel(x_ref, gc_ref, ga_ref, cb_ref, cc_ref, cv_ref, ccp_ref, cvp_ref, at_ref,
                cw_ref, bg_ref, wco_ref, wao_ref, wmo_ref, o_ref, ext_ref):
    i = pl.program_id(0)
    tm = x_ref.shape[0]
    halo = ccp_ref.shape[0]
    d = x_ref.shape[1]

    prev = ccp_ref[...].astype(F32) * cvp_ref[...].astype(F32)
    ext_ref[0:halo, :] = jnp.where(i == 0, 0.0, prev)
    ext_ref[halo:, :] = cc_ref[...].astype(F32) * cv_ref[...].astype(F32)
    y = None
    for tap in range(CONV_K):
        shifted = ext_ref[pl.ds(halo - (CONV_K - 1) + tap, tm), :]
        term = cw_ref[tap:tap + 1, :] * shifted
        y = term if y is None else y + term
    conv_y = (cb_ref[...].astype(F32) * y).astype(BF16)

    conv_branch = jnp.dot(conv_y, wco_ref[...], preferred_element_type=F32)
    attn_branch = jnp.dot(at_ref[...], wao_ref[...], preferred_element_type=F32)
    gate_c = jax.nn.sigmoid(gc_ref[...].astype(F32) + bg_ref[:, 0:d])
    gate_a = jax.nn.sigmoid(ga_ref[...].astype(F32) + bg_ref[:, d:])
    merged = (gate_c * conv_branch + gate_a * attn_branch).astype(BF16)
    o_ref[...] = x_ref[...] + jnp.dot(merged, wmo_ref[...], preferred_element_type=F32)


def _mix(x, z, attn, conv_w, b_gate, wco, wao, wmo):
    s, d = x.shape
    c = wco.shape[0]
    a = wao.shape[0]
    assert d == 2 * c and z.shape[1] == 2 * d + 3 * c
    tm = MIX_TM
    halo = V7X_BF16_SUBLANES
    assert s % tm == 0 and tm % halo == 0 and halo >= CONV_K - 1
    rows_per_tile = tm // halo
    prev_rows = lambda i: jnp.maximum(i * rows_per_tile - 1, 0)
    vmem = (4 * tm * d * 4 + 4 * tm * d * 2 + 6 * tm * c * 2 + 2 * tm * a * 2
            + (c + a + d) * d * 2 + (tm + halo) * c * 4
            + 4 * tm * d * 4 + 3 * tm * c * 4 + 4 * MIB)
    return pl.pallas_call(
        _mix_kernel,
        grid=(s // tm,),
        in_specs=[pl.BlockSpec((tm, d), lambda i: (i, 0)),
                  pl.BlockSpec((tm, d), lambda i: (i, 0)),
                  pl.BlockSpec((tm, d), lambda i: (i, 1)),
                  pl.BlockSpec((tm, c), lambda i: (i, 4)),
                  pl.BlockSpec((tm, c), lambda i: (i, 5)),
                  pl.BlockSpec((tm, c), lambda i: (i, 6)),
                  pl.BlockSpec((halo, c), lambda i: (prev_rows(i), 5)),
                  pl.BlockSpec((halo, c), lambda i: (prev_rows(i), 6)),
                  pl.BlockSpec((tm, a), lambda i: (i, 0)),
                  _resident((CONV_K, c), lambda i: (0, 0)),
                  _resident((1, 2 * d), lambda i: (0, 0)),
                  _resident((c, d), lambda i: (0, 0)),
                  _resident((a, d), lambda i: (0, 0)),
                  _resident((d, d), lambda i: (0, 0))],
        out_specs=pl.BlockSpec((tm, d), lambda i: (i, 0)),
        out_shape=jax.ShapeDtypeStruct((s, d), F32),
        scratch_shapes=[pltpu.VMEM((tm + halo, c), F32)],
        compiler_params=_params(("arbitrary",), vmem),
        name="mix",
    )(x, z, z, z, z, z, z, z, attn, conv_w, b_gate, wco, wao, wmo)


def _mlp_kernel(x_ref, g_ref, w1_ref, w2_ref, gf_ref, o_ref, h_ref, *, final_norm):
    j = pl.program_id(1)

    @pl.when(j == 0)
    def _():
        x = x_ref[...]
        h_ref[...] = _rmsnorm(x, g_ref[...]).astype(BF16)
        o_ref[...] = x

    a = jnp.dot(h_ref[...], w1_ref[...], preferred_element_type=F32)
    u = jnp.square(jnp.maximum(a, 0.0)).astype(BF16)
    o_ref[...] += jnp.dot(u, w2_ref[...], preferred_element_type=F32)

    if final_norm:
        @pl.when(j == pl.num_programs(1) - 1)
        def _():
            o_ref[...] = _rmsnorm(o_ref[...], gf_ref[...])


def _mlp(x, g, w1, w2, g_final, final_norm):
    s, d = x.shape
    ff = w1.shape[1]
    tm, tf = MLP_TM, MLP_TF
    assert s % tm == 0 and ff % tf == 0
    vmem = (4 * tm * d * 4 + tm * d * 2 + 4 * d * tf * 2
            + tm * tf * 6 + 2 * tm * d * 4 + 4 * MIB)
    return pl.pallas_call(
        functools.partial(_mlp_kernel, final_norm=final_norm),
        grid=(s // tm, ff // tf),
        in_specs=[pl.BlockSpec((tm, d), lambda i, j: (i, 0)),
                  pl.BlockSpec((1, d), lambda i, j: (0, 0)),
                  pl.BlockSpec((d, tf), lambda i, j: (0, j)),
                  pl.BlockSpec((tf, d), lambda i, j: (j, 0)),
                  pl.BlockSpec((1, d), lambda i, j: (0, 0))],
        out_specs=pl.BlockSpec((tm, d), lambda i, j: (i, 0)),
        out_shape=jax.ShapeDtypeStruct((s, d), F32),
        scratch_shapes=[pltpu.VMEM((tm, d), BF16)],
        compiler_params=_params(("arbitrary", "arbitrary"), vmem),
        name="mlp",
    )(x, g, w1, w2, g_final)


def kernel(x, g_mix, w_in, b_f, b_gate, conv_w, w_conv_out, w_attn_out, w_mix_out, g_mlp, w_ff1, w_ff2, g_final):
    b, s, d = x.shape
    depth = g_mix.shape[0]
    c = w_conv_out.shape[1]
    a = w_attn_out.shape[1]
    n_heads = b_f.shape[1]
    assert a == n_heads * HEAD_DIM and w_in.shape[2] == 3 * c + 3 * a + n_heads + 2 * d
    o_q, o_k, o_v, o_f, o_g = 3 * c, 3 * c + a, 3 * c + 2 * a, 3 * c + 3 * a, 3 * c + 3 * a + n_heads

    outs = []
    for bi in range(b):
        xs = x[bi]
        for l in range(depth):
            w = w_in[l]
            w_cg = jnp.concatenate([w[:, o_g:], w[:, :o_q]], axis=1).astype(BF16)
            wqt = w[:, o_q:o_k].T.astype(BF16)
            wk = w[:, o_k:o_v].astype(BF16)
            wvt = w[:, o_v:o_f].T.astype(BF16)
            wf = jnp.pad(w[:, o_f:o_g], ((0, 0), (0, V7X_LANES - n_heads))).astype(BF16)
            bf = jnp.pad(b_f[l], (0, V7X_LANES - n_heads)).reshape(1, V7X_LANES)
            g1 = g_mix[l].reshape(1, d)

            z = _inproj(xs, g1, w_cg)
            qt, k, vt, r = _attn_prep(xs, g1, wqt, wk, wvt, wf, bf)
            r_heads = r[:, 0, :n_heads].T
            attn = _flash(r_heads, qt, k, vt)
            xs = _mix(xs, z, attn, conv_w[l], b_gate[l].reshape(1, 2 * d),
                      w_conv_out[l].astype(BF16), w_attn_out[l].astype(BF16), w_mix_out[l].astype(BF16))
            xs = _mlp(xs, g_mlp[l].reshape(1, d), w_ff1[l].astype(BF16), w_ff2[l].astype(BF16),
                      g_final.reshape(1, d), final_norm=(l == depth - 1))
        outs.append(xs)
    return jnp.stack(outs, axis=0)
```

```python
import functools

import jax
import jax.numpy as jnp
from jax import lax
from jax.experimental import pallas as pl
from jax.experimental.pallas import tpu as pltpu

EPS = 1e-6
HEAD_DIM = 128
CONV_K = 3

V7X_LANES = 128
V7X_MXU_DEPTH = 256
V7X_F32_SUBLANES = 8
V7X_BF16_SUBLANES = 16
V7X_VMEM_BYTES = 64 * 1024 * 1024
MIB = 1024 * 1024

INPROJ_TM = 1024
INPROJ_TN = 1024
KV_BLOCK = 512
Q_BLOCK = 2 * KV_BLOCK
MIX_TM = 256
MLP_TM = 512
MLP_TF = 1024
SUM_ROWS = V7X_BF16_SUBLANES
V_ROWS = HEAD_DIM + SUM_ROWS

MASK_VALUE = -1e30
LOG2_E = 1.4426950408889634

BF16 = jnp.bfloat16
F32 = jnp.float32


def _rmsnorm(x, g):
    ms = jnp.mean(x * x, axis=-1, keepdims=True)
    return x * lax.rsqrt(ms + EPS) * g


def _split3_bf16(v):
    hi = v.astype(BF16).astype(F32)
    r1 = v - hi
    mid = r1.astype(BF16).astype(F32)
    lo = (r1 - mid).astype(BF16).astype(F32)
    return hi, mid, lo


def _params(semantics, vmem_bytes, flags=None):
    assert vmem_bytes <= V7X_VMEM_BYTES - 4 * MIB, vmem_bytes
    return pltpu.CompilerParams(dimension_semantics=semantics, vmem_limit_bytes=int(vmem_bytes), flags=flags)


def _resident(block_shape, index_map):
    return pl.BlockSpec(block_shape, index_map, pipeline_mode=pl.Buffered(1))


def _inproj_kernel(x_ref, g_ref, w_ref, z_ref, h_ref):
    @pl.when(pl.program_id(1) == 0)
    def _():
        h_ref[...] = _rmsnorm(x_ref[...], g_ref[...]).astype(BF16)

    z_ref[...] = jnp.dot(h_ref[...], w_ref[...], preferred_element_type=F32).astype(z_ref.dtype)


def _inproj(x, g, w):
    s, d = x.shape
    n = w.shape[1]
    tm, tn = min(INPROJ_TM, s), INPROJ_TN
    assert s % tm == 0 and n % tn == 0
    vmem = (2 * tm * d * 4 + tm * d * 2 + 2 * d * tn * 2 + 2 * tm * tn * 2
            + tm * tn * 4 + tm * d * 4 + 4 * MIB)
    return pl.pallas_call(
        _inproj_kernel,
        grid=(s // tm, n // tn),
        in_specs=[pl.BlockSpec((tm, d), lambda i, j: (i, 0)),
                  pl.BlockSpec((1, d), lambda i, j: (0, 0)),
                  pl.BlockSpec((d, tn), lambda i, j: (0, j))],
        out_specs=pl.BlockSpec((tm, tn), lambda i, j: (i, j)),
        out_shape=jax.ShapeDtypeStruct((s, n), BF16),
        scratch_shapes=[pltpu.VMEM((tm, d), BF16)],
        compiler_params=_params(("arbitrary", "arbitrary"), vmem),
        name="inproj",
    )(x, g, w)


def _prep_kernel(x_ref, g_ref, wqt_ref, wk_ref, wvt_ref, wf_ref, bf_ref,
                 qt_ref, k_ref, vt_ref, r_ref, carry_ref, *, n_heads, scale):
    i = pl.program_id(0)
    tm = x_ref.shape[0]

    @pl.when(i == 0)
    def _():
        carry_ref[...] = jnp.zeros_like(carry_ref)

    h = _rmsnorm(x_ref[...], g_ref[...]).astype(BF16)
    nt = (((1,), (1,)), ((), ()))
    qt = lax.dot_general(wqt_ref[...], h, nt, preferred_element_type=F32) * scale
    vt = lax.dot_general(wvt_ref[...], h, nt, preferred_element_type=F32)
    k = jnp.dot(h, wk_ref[...], preferred_element_type=F32)
    f = jnp.dot(h, wf_ref[...], preferred_element_type=F32) + bf_ref[...]

    lane = lax.broadcasted_iota(jnp.int32, f.shape, 1)
    logf = jnp.where(lane < n_heads, jax.nn.log_sigmoid(f) * LOG2_E, 0.0)

    row = lax.broadcasted_iota(jnp.int32, (tm, tm), 0)
    col = lax.broadcasted_iota(jnp.int32, (tm, tm), 1)
    tril = (col <= row).astype(BF16)
    c_rel = None
    for part in _split3_bf16(logf):
        term = jnp.dot(tril, part.astype(BF16), preferred_element_type=F32)
        c_rel = term if c_rel is None else c_rel + term

    carry = carry_ref[...]
    r_ref[0] = carry
    carry_ref[...] = carry + c_rel[tm - 1:tm, :]

    nhi, nmid, nlo = _split3_bf16(-c_rel)
    bias_cols = (nhi + pltpu.roll(nmid, n_heads, 1) + pltpu.roll(nlo, 2 * n_heads, 1)).astype(BF16)

    sel_row = lax.broadcasted_iota(jnp.int32, (V7X_LANES, tm), 0)
    sum_rows = (lax.broadcasted_iota(jnp.int32, (SUM_ROWS, tm), 0) == 0).astype(BF16)
    for hh in range(n_heads):
        sl = slice(hh * HEAD_DIM, (hh + 1) * HEAD_DIM)
        onehot = ((sel_row == hh) | (sel_row == n_heads + hh) | (sel_row == 2 * n_heads + hh))
        qt_ref[hh, 0:HEAD_DIM, :] = qt[sl, :].astype(BF16)
        qt_ref[hh, HEAD_DIM:, :] = onehot.astype(BF16)
        k_ref[hh, 0, :, 0:HEAD_DIM] = k[:, sl].astype(BF16)
        k_ref[hh, 0, :, HEAD_DIM:] = bias_cols
        vt_ref[hh, 0, 0:HEAD_DIM, :] = vt[sl, :].astype(BF16)
        vt_ref[hh, 0, HEAD_DIM:, :] = sum_rows


def _attn_prep(x, g, wqt, wk, wvt, wf, bf):
    s, d = x.shape
    a = wk.shape[1]
    n_heads = a // HEAD_DIM
    assert 3 * n_heads <= V7X_LANES and 2 * HEAD_DIM == V7X_MXU_DEPTH
    tm = KV_BLOCK
    assert s % tm == 0
    nb = s // tm
    vmem = (2 * tm * d * 4 + 3 * d * a * 2 + d * V7X_LANES * 2
            + 2 * n_heads * tm * (2 * V7X_MXU_DEPTH + V_ROWS) * 2
            + tm * d * 6 + 3 * tm * a * 4 + tm * tm * 2 + 6 * MIB)
    kern = functools.partial(_prep_kernel, n_heads=n_heads, scale=LOG2_E * float(HEAD_DIM) ** -0.5)
    return pl.pallas_call(
        kern,
        grid=(nb,),
        in_specs=[pl.BlockSpec((tm, d), lambda i: (i, 0)),
                  _resident((1, d), lambda i: (0, 0)),
                  _resident((a, d), lambda i: (0, 0)),
                  _resident((d, a), lambda i: (0, 0)),
                  _resident((a, d), lambda i: (0, 0)),
                  _resident((d, V7X_LANES), lambda i: (0, 0)),
                  _resident((1, V7X_LANES), lambda i: (0, 0))],
        out_specs=[pl.BlockSpec((n_heads, V7X_MXU_DEPTH, tm), lambda i: (0, 0, i)),
                   pl.BlockSpec((n_heads, 1, tm, V7X_MXU_DEPTH), lambda i: (0, i, 0, 0)),
                   pl.BlockSpec((n_heads, 1, V_ROWS, tm), lambda i: (0, i, 0, 0)),
                   pl.BlockSpec((1, V7X_F32_SUBLANES, V7X_LANES), lambda i: (i, 0, 0))],
        out_shape=[jax.ShapeDtypeStruct((n_heads, V7X_MXU_DEPTH, s), BF16),
                   jax.ShapeDtypeStruct((n_heads, nb, tm, V7X_MXU_DEPTH), BF16),
                   jax.ShapeDtypeStruct((n_heads, nb, V_ROWS, tm), BF16),
                   jax.ShapeDtypeStruct((nb, V7X_F32_SUBLANES, V7X_LANES), F32)],
        scratch_shapes=[pltpu.VMEM((V7X_F32_SUBLANES, V7X_LANES), F32)],
        compiler_params=_params(("arbitrary",), vmem),
        name="attn_prep",
    )(x, g, wqt, wk, wvt, wf, bf)


def _flash_kernel(r_ref, qt_ref, k_ref, vt_ref, o_ref, s_ref, acc_ref):
    hd = pl.program_id(0)
    qi = pl.program_id(1)
    tk = k_ref.shape[1]
    tq = qt_ref.shape[1]

    sw = V7X_MXU_DEPTH
    strips = [slice(c * sw, (c + 1) * sw) for c in range(tq // sw)]

    def scores(blk, c):
        s = jnp.dot(k_ref[blk], qt_ref[:, strips[c]], preferred_element_type=F32)
        s_ref[:, strips[c]] = s
        return jnp.max(s, axis=0, keepdims=True)

    def attend(blk, c, m, bm, key0=None):
        s = s_ref[:, strips[c]]
        if key0 is not None:
            key = key0 + lax.broadcasted_iota(jnp.int32, (tk, sw), 0)
            qry = c * sw + lax.broadcasted_iota(jnp.int32, (tk, sw), 1)
            s = jnp.where(key <= qry, s, MASK_VALUE)
            bm = jnp.max(s, axis=0, keepdims=True)
        rb = r_ref[hd, blk]
        m_new = jnp.maximum(m, bm - rb)
        alpha = jnp.exp2(m - m_new)
        p = jnp.exp2(s - (m_new + rb)).astype(BF16)
        acc_ref[:, strips[c]] = alpha * acc_ref[:, strips[c]] + jnp.dot(
            vt_ref[blk], p, preferred_element_type=F32)
        return m_new

    def visible(key0, c):
        if key0 is None or key0 + tk - 1 <= c * sw:
            return "all"
        return "none" if key0 > (c + 1) * sw - 1 else "some"

    def block(t, carry, key0=None, next_key0=None, last=False):
        out = []
        for c in range(len(strips)):
            m, bm = carry[c]
            if visible(key0, c) != "none":
                m = attend(t, c, m, bm, key0 if visible(key0, c) == "some" else None)
            if not last and visible(next_key0, c) != "none":
                bm = scores(t + 1, c)
            out.append((m, bm))
        return tuple(out)

    acc_ref[...] = jnp.zeros_like(acc_ref)
    init = tuple((jnp.full((1, sw), MASK_VALUE, F32), scores(0, c)) for c in range(len(strips)))

    def blocks(first, count, carry):
        for j in range(count):
            carry = block(first + j, carry)
        return carry

    carry = lax.fori_loop(0, qi // 2, lambda w, carry: blocks(4 * w, 4, carry), init)
    carry = lax.cond(qi % 2 == 1, lambda carry: blocks(2 * qi - 2, 2, carry), lambda carry: carry, carry)

    carry = block(2 * qi, carry, key0=0, next_key0=tk)
    block(2 * qi + 1, carry, key0=tk, last=True)
    inv_l = 1.0 / acc_ref[HEAD_DIM:HEAD_DIM + 1, :]
    o_ref[...] = (acc_ref[0:HEAD_DIM, :] * inv_l).T.astype(o_ref.dtype)


def _flash(r, qt, k, vt):
    n_heads, depth, s = qt.shape
    nb, tk = k.shape[1], k.shape[2]
    tq = Q_BLOCK
    assert tq == 2 * tk and s % tq == 0
    v_rows = vt.shape[2]
    vmem = (2 * nb * tk * (depth + v_rows) * 2 + 2 * depth * tq * 2 + 2 * tq * HEAD_DIM * 2
            + 2 * tk * tq * 6 + v_rows * tq * 4 + 3 * tk * tq * 4 + 4 * MIB)
    return pl.pallas_call(
        _flash_kernel,
        grid=(n_heads, s // tq),
        in_specs=[pl.BlockSpec(memory_space=pltpu.SMEM),
                  pl.BlockSpec((None, depth, tq), lambda h, i: (h, 0, i)),
                  pl.BlockSpec((None, nb, tk, depth), lambda h, i: (h, 0, 0, 0)),
                  pl.BlockSpec((None, nb, v_rows, tk), lambda h, i: (h, 0, 0, 0))],
        out_specs=pl.BlockSpec((tq, HEAD_DIM), lambda h, i: (i, h)),
        out_shape=jax.ShapeDtypeStruct((s, n_heads * HEAD_DIM), BF16),
        scratch_shapes=[pltpu.VMEM((tk, tq), F32),
                        pltpu.VMEM((v_rows, tq), F32)],
        compiler_params=_params(("arbitrary", "arbitrary"), vmem),
        name="flash",
    )(r, qt, k, vt)


def _mix_kernel(x_ref, gc_ref, ga_ref, cb_ref, cc_ref, cv_ref, ccp_ref, cvp_ref, at_ref,
                cw_ref, bg_ref, wco_ref, wao_ref, wmo_ref, o_ref, ext_ref):
    i = pl.program_id(0)
    tm = x_ref.shape[0]
    halo = ccp_ref.shape[0]
    d = x_ref.shape[1]

    prev = ccp_ref[...].astype(F32) * cvp_ref[...].astype(F32)
    ext_ref[0:halo, :] = jnp.where(i == 0, 0.0, prev)
    ext_ref[halo:, :] = cc_ref[...].astype(F32) * cv_ref[...].astype(F32)
    y = None
    for tap in range(CONV_K):
        shifted = ext_ref[pl.ds(halo - (CONV_K - 1) + tap, tm), :]
        term = cw_ref[tap:tap + 1, :] * shifted
        y = term if y is None else y + term
    conv_y = (cb_ref[...].astype(F32) * y).astype(BF16)

    conv_branch = jnp.dot(conv_y, wco_ref[...], preferred_element_type=F32)
    attn_branch = jnp.dot(at_ref[...], wao_ref[...], preferred_element_type=F32)
    gate_c = jax.nn.sigmoid(gc_ref[...].astype(F32) + bg_ref[:, 0:d])
    gate_a = jax.nn.sigmoid(ga_ref[...].astype(F32) + bg_ref[:, d:])
    merged = (gate_c * conv_branch + gate_a * attn_branch).astype(BF16)
    o_ref[...] = x_ref[...] + jnp.dot(merged, wmo_ref[...], preferred_element_type=F32)


def _mix(x, z, attn, conv_w, b_gate, wco, wao, wmo):
    s, d = x.shape
    c = wco.shape[0]
    a = wao.shape[0]
    assert d == 2 * c and z.shape[1] == 2 * d + 3 * c
    tm = MIX_TM
    halo = V7X_BF16_SUBLANES
    assert s % tm == 0 and tm % halo == 0 and halo >= CONV_K - 1
    rows_per_tile = tm // halo
    prev_rows = lambda i: jnp.maximum(i * rows_per_tile - 1, 0)
    vmem = (4 * tm * d * 4 + 4 * tm * d * 2 + 6 * tm * c * 2 + 2 * tm * a * 2
            + (c + a + d) * d * 2 + (tm + halo) * c * 4
            + 4 * tm * d * 4 + 3 * tm * c * 4 + 4 * MIB)
    return pl.pallas_call(
        _mix_kernel,
        grid=(s // tm,),
        in_specs=[pl.BlockSpec((tm, d), lambda i: (i, 0)),
                  pl.BlockSpec((tm, d), lambda i: (i, 0)),
                  pl.BlockSpec((tm, d), lambda i: (i, 1)),
                  pl.BlockSpec((tm, c), lambda i: (i, 4)),
                  pl.BlockSpec((tm, c), lambda i: (i, 5)),
                  pl.BlockSpec((tm, c), lambda i: (i, 6)),
                  pl.BlockSpec((halo, c), lambda i: (prev_rows(i), 5)),
                  pl.BlockSpec((halo, c), lambda i: (prev_rows(i), 6)),
                  pl.BlockSpec((tm, a), lambda i: (i, 0)),
                  _resident((CONV_K, c), lambda i: (0, 0)),
                  _resident((1, 2 * d), lambda i: (0, 0)),
                  _resident((c, d), lambda i: (0, 0)),
                  _resident((a, d), lambda i: (0, 0)),
                  _resident((d, d), lambda i: (0, 0))],
        out_specs=pl.BlockSpec((tm, d), lambda i: (i, 0)),
        out_shape=jax.ShapeDtypeStruct((s, d), F32),
        scratch_shapes=[pltpu.VMEM((tm + halo, c), F32)],
        compiler_params=_params(("arbitrary",), vmem),
        name="mix",
    )(x, z, z, z, z, z, z, z, attn, conv_w, b_gate, wco, wao, wmo)


def _mlp_kernel(x_ref, g_ref, w1_ref, w2_ref, gf_ref, o_ref, h_ref, *, final_norm):
    j = pl.program_id(1)

    @pl.when(j == 0)
    def _():
        x = x_ref[...]
        h_ref[...] = _rmsnorm(x, g_ref[...]).astype(BF16)
        o_ref[...] = x

    a = jnp.dot(h_ref[...], w1_ref[...], preferred_element_type=F32)
    u = jnp.square(jnp.maximum(a, 0.0)).astype(BF16)
    o_ref[...] += jnp.dot(u, w2_ref[...], preferred_element_type=F32)

    if final_norm:
        @pl.when(j == pl.num_programs(1) - 1)
        def _():
            o_ref[...] = _rmsnorm(o_ref[...], gf_ref[...])


def _mlp(x, g, w1, w2, g_final, final_norm):
    s, d = x.shape
    ff = w1.shape[1]
    tm, tf = MLP_TM, MLP_TF
    assert s % tm == 0 and ff % tf == 0
    vmem = (4 * tm * d * 4 + tm * d * 2 + 4 * d * tf * 2
            + tm * tf * 6 + 2 * tm * d * 4 + 4 * MIB)
    return pl.pallas_call(
        functools.partial(_mlp_kernel, final_norm=final_norm),
        grid=(s // tm, ff // tf),
        in_specs=[pl.BlockSpec((tm, d), lambda i, j: (i, 0)),
                  pl.BlockSpec((1, d), lambda i, j: (0, 0)),
                  pl.BlockSpec((d, tf), lambda i, j: (0, j)),
                  pl.BlockSpec((tf, d), lambda i, j: (j, 0)),
                  pl.BlockSpec((1, d), lambda i, j: (0, 0))],
        out_specs=pl.BlockSpec((tm, d), lambda i, j: (i, 0)),
        out_shape=jax.ShapeDtypeStruct((s, d), F32),
        scratch_shapes=[pltpu.VMEM((tm, d), BF16)],
        compiler_params=_params(("arbitrary", "arbitrary"), vmem),
        name="mlp",
    )(x, g, w1, w2, g_final)


def kernel(x, g_mix, w_in, b_f, b_gate, conv_w, w_conv_out, w_attn_out, w_mix_out, g_mlp, w_ff1, w_ff2, g_final):
    b, s, d = x.shape
    depth = g_mix.shape[0]
    c = w_conv_out.shape[1]
    a = w_attn_out.shape[1]
    n_heads = b_f.shape[1]
    assert a == n_heads * HEAD_DIM and w_in.shape[2] == 3 * c + 3 * a + n_heads + 2 * d
    o_q, o_k, o_v, o_f, o_g = 3 * c, 3 * c + a, 3 * c + 2 * a, 3 * c + 3 * a, 3 * c + 3 * a + n_heads

    outs = []
    for bi in range(b):
        xs = x[bi]
        for l in range(depth):
            w = w_in[l]
            w_cg = jnp.concatenate([w[:, o_g:], w[:, :o_q]], axis=1).astype(BF16)
            wqt = w[:, o_q:o_k].T.astype(BF16)
            wk = w[:, o_k:o_v].astype(BF16)
            wvt = w[:, o_v:o_f].T.astype(BF16)
            wf = jnp.pad(w[:, o_f:o_g], ((0, 0), (0, V7X_LANES - n_heads))).astype(BF16)
            bf = jnp.pad(b_f[l], (0, V7X_LANES - n_heads)).reshape(1, V7X_LANES)
            g1 = g_mix[l].reshape(1, d)

            z = _inproj(xs, g1, w_cg)
            qt, k, vt, r = _attn_prep(xs, g1, wqt, wk, wvt, wf, bf)
            r_heads = r[:, 0, :n_heads].T
            attn = _flash(r_heads, qt, k, vt)
            xs = _mix(xs, z, attn, conv_w[l], b_gate[l].reshape(1, 2 * d),
                      w_conv_out[l].astype(BF16), w_attn_out[l].astype(BF16), w_mix_out[l].astype(BF16))
            xs = _mlp(xs, g_mlp[l].reshape(1, d), w_ff1[l].astype(BF16), w_ff2[l].astype(BF16),
                      g_final.reshape(1, d), final_norm=(l == depth - 1))
        outs.append(xs)
    return jnp.stack(outs, axis=0)
```

```python
import functools

import jax
import jax.numpy as jnp
from jax import lax
from jax.experimental import pallas as pl
from jax.experimental.pallas import tpu as pltpu

EPS = 1e-6
HEAD_DIM = 128
CONV_K = 3

V7X_LANES = 128
V7X_MXU_DEPTH = 256
V7X_F32_SUBLANES = 8
V7X_BF16_SUBLANES = 16
V7X_VMEM_BYTES = 64 * 1024 * 1024
MIB = 1024 * 1024

INPROJ_TM = 1024
INPROJ_TN = 1024
KV_BLOCK = 512
Q_BLOCK = 2 * KV_BLOCK
MIX_TM = 256
MLP_TM = 512
MLP_TF = 1024
CAST_BLOCK_BYTES = 8 * MIB
SUM_ROWS = V7X_BF16_SUBLANES
V_ROWS = HEAD_DIM + SUM_ROWS

MASK_VALUE = -1e30
LOG2_E = 1.4426950408889634

BF16 = jnp.bfloat16
F32 = jnp.float32


def _rmsnorm(x, g):
    ms = jnp.mean(x * x, axis=-1, keepdims=True)
    return x * lax.rsqrt(ms + EPS) * g


def _split3_bf16(v):
    hi = v.astype(BF16).astype(F32)
    r1 = v - hi
    mid = r1.astype(BF16).astype(F32)
    lo = (r1 - mid).astype(BF16).astype(F32)
    return hi, mid, lo


def _params(semantics, vmem_bytes, flags=None):
    assert vmem_bytes <= V7X_VMEM_BYTES - 4 * MIB, vmem_bytes
    return pltpu.CompilerParams(dimension_semantics=semantics, vmem_limit_bytes=int(vmem_bytes), flags=flags)


def _resident(block_shape, index_map):
    return pl.BlockSpec(block_shape, index_map, pipeline_mode=pl.Buffered(1))


def _cast_kernel(w_ref, o_ref):
    o_ref[...] = w_ref[...].astype(o_ref.dtype)


def _cast_bf16(w, width=None):
    depth, r, c = w.shape
    if width is not None:
        assert width % V7X_LANES == 0 and width <= c
        c = width
    tr = r
    while tr * c * 4 > CAST_BLOCK_BYTES and tr % (2 * V7X_BF16_SUBLANES) == 0:
        tr //= 2
    vmem = 2 * tr * c * 6 + 4 * MIB
    return pl.pallas_call(
        _cast_kernel,
        grid=(depth, r // tr),
        in_specs=[pl.BlockSpec((None, tr, c), lambda l, i: (l, i, 0))],
        out_specs=pl.BlockSpec((None, tr, c), lambda l, i: (l, i, 0)),
        out_shape=jax.ShapeDtypeStruct((depth, r, c), BF16),
        compiler_params=_params(("arbitrary", "arbitrary"), vmem),
        name="cast_bf16",
    )(w)


def _qkvf_kernel(w_ref, wf_ref, wqt_ref, wk_ref, wvt_ref, wfo_ref, *, chunks):
    c = pl.program_id(1)
    w = w_ref[...]

    @pl.when(c < chunks)
    def _():
        wqt_ref[...] = w.T.astype(BF16)

    @pl.when((c >= chunks) & (c < 2 * chunks))
    def _():
        wk_ref[...] = w.astype(BF16)

    @pl.when(c >= 2 * chunks)
    def _():
        wvt_ref[...] = w.T.astype(BF16)

    @pl.when(c == 0)
    def _():
        wfo_ref[...] = wf_ref[...].astype(BF16)


def _qkvf_weights(w_in, o_q, o_f, a):
    depth, d, _ = w_in.shape
    cw = V7X_MXU_DEPTH
    chunks = a // cw
    assert o_q % cw == 0 and a % cw == 0 and o_f % V7X_LANES == 0 and o_f == o_q + 3 * a
    q0, f0 = o_q // cw, o_f // V7X_LANES
    clip = lambda c, lo: jnp.clip(c - lo, 0, chunks - 1)
    vmem = 2 * d * cw * 4 + 6 * d * cw * 2 + 4 * d * V7X_LANES * 4 + 2 * d * cw * 4 + 4 * MIB
    return pl.pallas_call(
        functools.partial(_qkvf_kernel, chunks=chunks),
        grid=(depth, 3 * chunks),
        in_specs=[pl.BlockSpec((None, d, cw), lambda l, c: (l, 0, q0 + c)),
                  pl.BlockSpec((None, d, V7X_LANES), lambda l, c: (l, 0, f0))],
        out_specs=[pl.BlockSpec((None, cw, d), lambda l, c: (l, clip(c, 0), 0)),
                   pl.BlockSpec((None, d, cw), lambda l, c: (l, 0, clip(c, chunks))),
                   pl.BlockSpec((None, cw, d), lambda l, c: (l, clip(c, 2 * chunks), 0)),
                   pl.BlockSpec((None, d, V7X_LANES), lambda l, c: (l, 0, 0))],
        out_shape=[jax.ShapeDtypeStruct((depth, a, d), BF16),
                   jax.ShapeDtypeStruct((depth, d, a), BF16),
                   jax.ShapeDtypeStruct((depth, a, d), BF16),
                   jax.ShapeDtypeStruct((depth, d, V7X_LANES), BF16)],
        compiler_params=_params(("arbitrary", "arbitrary"), vmem),
        name="qkvf_weights",
    )(w_in, w_in)


def _inproj_kernel(x_ref, g_ref, wa_ref, wb_ref, z_ref, h_ref, *, a_tiles):
    j = pl.program_id(1)

    @pl.when(j == 0)
    def _():
        h_ref[...] = _rmsnorm(x_ref[...], g_ref[...]).astype(BF16)

    @pl.when(j < a_tiles)
    def _():
        z_ref[...] = jnp.dot(h_ref[...], wa_ref[...], preferred_element_type=F32).astype(z_ref.dtype)

    @pl.when(j >= a_tiles)
    def _():
        z_ref[...] = jnp.dot(h_ref[...], wb_ref[...], preferred_element_type=F32).astype(z_ref.dtype)


def _inproj(x, g, wa, wb, layer):
    s, d = x.shape
    na, nb = wa.shape[2], wb.shape[2]
    tm, tn = min(INPROJ_TM, s), INPROJ_TN
    assert s % tm == 0 and na % tn == 0 and nb % tn == 0
    a_tiles = na // tn
    vmem = (2 * tm * d * 4 + tm * d * 2 + 4 * d * tn * 2 + 2 * tm * tn * 2
            + tm * tn * 4 + tm * d * 4 + 4 * MIB)
    return pl.pallas_call(
        functools.partial(_inproj_kernel, a_tiles=a_tiles),
        grid=(s // tm, (na + nb) // tn),
        in_specs=[pl.BlockSpec((tm, d), lambda i, j: (i, 0)),
                  pl.BlockSpec((1, d), lambda i, j: (0, 0)),
                  pl.BlockSpec((None, d, tn), lambda i, j: (layer, 0, jnp.minimum(j, a_tiles - 1))),
                  pl.BlockSpec((None, d, tn), lambda i, j: (layer, 0, jnp.maximum(j - a_tiles, 0)))],
        out_specs=pl.BlockSpec((tm, tn), lambda i, j: (i, j)),
        out_shape=jax.ShapeDtypeStruct((s, na + nb), BF16),
        scratch_shapes=[pltpu.VMEM((tm, d), BF16)],
        compiler_params=_params(("arbitrary", "arbitrary"), vmem),
        name="inproj",
    )(x, g, wa, wb)


def _prep_kernel(x_ref, g_ref, wqt_ref, wk_ref, wvt_ref, wf_ref, bf_ref,
                 qt_ref, k_ref, vt_ref, r_ref, carry_ref, *, n_heads, scale):
    i = pl.program_id(0)
    tm = x_ref.shape[0]

    @pl.when(i == 0)
    def _():
        carry_ref[...] = jnp.zeros_like(carry_ref)

    h = _rmsnorm(x_ref[...], g_ref[...]).astype(BF16)
    nt = (((1,), (1,)), ((), ()))
    qt = lax.dot_general(wqt_ref[...], h, nt, preferred_element_type=F32) * scale
    vt = lax.dot_general(wvt_ref[...], h, nt, preferred_element_type=F32)
    k = jnp.dot(h, wk_ref[...], preferred_element_type=F32)
    f = jnp.dot(h, wf_ref[...], preferred_element_type=F32) + bf_ref[...]

    lane = lax.broadcasted_iota(jnp.int32, f.shape, 1)
    logf = jnp.where(lane < n_heads, jax.nn.log_sigmoid(f) * LOG2_E, 0.0)

    row = lax.broadcasted_iota(jnp.int32, (tm, tm), 0)
    col = lax.broadcasted_iota(jnp.int32, (tm, tm), 1)
    tril = (col <= row).astype(BF16)
    c_rel = None
    for part in _split3_bf16(logf):
        term = jnp.dot(tril, part.astype(BF16), preferred_element_type=F32)
        c_rel = term if c_rel is None else c_rel + term

    carry = carry_ref[...]
    r_ref[0] = carry
    carry_ref[...] = carry + c_rel[tm - 1:tm, :]

    nhi, nmid, nlo = _split3_bf16(-c_rel)
    bias_cols = (nhi + pltpu.roll(nmid, n_heads, 1) + pltpu.roll(nlo, 2 * n_heads, 1)).astype(BF16)

    sel_row = lax.broadcasted_iota(jnp.int32, (V7X_LANES, tm), 0)
    sum_rows = (lax.broadcasted_iota(jnp.int32, (SUM_ROWS, tm), 0) == 0).astype(BF16)
    for hh in range(n_heads):
        sl = slice(hh * HEAD_DIM, (hh + 1) * HEAD_DIM)
        onehot = ((sel_row == hh) | (sel_row == n_heads + hh) | (sel_row == 2 * n_heads + hh))
        qt_ref[hh, 0:HEAD_DIM, :] = qt[sl, :].astype(BF16)
        qt_ref[hh, HEAD_DIM:, :] = onehot.astype(BF16)
        k_ref[hh, 0, :, 0:HEAD_DIM] = k[:, sl].astype(BF16)
        k_ref[hh, 0, :, HEAD_DIM:] = bias_cols
        vt_ref[hh, 0, 0:HEAD_DIM, :] = vt[sl, :].astype(BF16)
        vt_ref[hh, 0, HEAD_DIM:, :] = sum_rows


def _attn_prep(x, g, wqt, wk, wvt, wf, bf, layer):
    s, d = x.shape
    a = wk.shape[2]
    n_heads = a // HEAD_DIM
    assert 3 * n_heads <= V7X_LANES and 2 * HEAD_DIM == V7X_MXU_DEPTH
    tm = KV_BLOCK
    assert s % tm == 0
    nb = s // tm
    vmem = (2 * tm * d * 4 + 3 * d * a * 2 + d * V7X_LANES * 2
            + 2 * n_heads * tm * (2 * V7X_MXU_DEPTH + V_ROWS) * 2
            + tm * d * 6 + 3 * tm * a * 4 + tm * tm * 2 + 6 * MIB)
    kern = functools.partial(_prep_kernel, n_heads=n_heads, scale=LOG2_E * float(HEAD_DIM) ** -0.5)
    return pl.pallas_call(
        kern,
        grid=(nb,),
        in_specs=[pl.BlockSpec((tm, d), lambda i: (i, 0)),
                  _resident((1, d), lambda i: (0, 0)),
                  _resident((None, a, d), lambda i: (layer, 0, 0)),
                  _resident((None, d, a), lambda i: (layer, 0, 0)),
                  _resident((None, a, d), lambda i: (layer, 0, 0)),
                  _resident((None, d, V7X_LANES), lambda i: (layer, 0, 0)),
                  _resident((1, V7X_LANES), lambda i: (0, 0))],
        out_specs=[pl.BlockSpec((n_heads, V7X_MXU_DEPTH, tm), lambda i: (0, 0, i)),
                   pl.BlockSpec((n_heads, 1, tm, V7X_MXU_DEPTH), lambda i: (0, i, 0, 0)),
                   pl.BlockSpec((n_heads, 1, V_ROWS, tm), lambda i: (0, i, 0, 0)),
                   pl.BlockSpec((1, V7X_F32_SUBLANES, V7X_LANES), lambda i: (i, 0, 0))],
        out_shape=[jax.ShapeDtypeStruct((n_heads, V7X_MXU_DEPTH, s), BF16),
                   jax.ShapeDtypeStruct((n_heads, nb, tm, V7X_MXU_DEPTH), BF16),
                   jax.ShapeDtypeStruct((n_heads, nb, V_ROWS, tm), BF16),
                   jax.ShapeDtypeStruct((nb, V7X_F32_SUBLANES, V7X_LANES), F32)],
        scratch_shapes=[pltpu.VMEM((V7X_F32_SUBLANES, V7X_LANES), F32)],
        compiler_params=_params(("arbitrary",), vmem),
        name="attn_prep",
    )(x, g, wqt, wk, wvt, wf, bf)


def _flash_kernel(r_ref, qt_ref, k_ref, vt_ref, o_ref, s_ref, acc_ref):
    hd = pl.program_id(0)
    qi = pl.program_id(1)
    tk = k_ref.shape[1]
    tq = qt_ref.shape[1]

    sw = V7X_MXU_DEPTH
    strips = [slice(c * sw, (c + 1) * sw) for c in range(tq // sw)]

    def scores(blk, c):
        s = jnp.dot(k_ref[blk], qt_ref[:, strips[c]], preferred_element_type=F32)
        s_ref[:, strips[c]] = s
        return jnp.max(s, axis=0, keepdims=True)

    def attend(blk, c, m, bm, key0=None):
        s = s_ref[:, strips[c]]
        if key0 is not None:
            key = key0 + lax.broadcasted_iota(jnp.int32, (tk, sw), 0)
            qry = c * sw + lax.broadcasted_iota(jnp.int32, (tk, sw), 1)
            s = jnp.where(key <= qry, s, MASK_VALUE)
            bm = jnp.max(s, axis=0, keepdims=True)
        rb = r_ref[hd, blk]
        m_new = jnp.maximum(m, bm - rb)
        alpha = jnp.exp2(m - m_new)
        p = jnp.exp2(s - (m_new + rb)).astype(BF16)
        acc_ref[:, strips[c]] = alpha * acc_ref[:, strips[c]] + jnp.dot(
            vt_ref[blk], p, preferred_element_type=F32)
        return m_new

    def visible(key0, c):
        if key0 is None or key0 + tk - 1 <= c * sw:
            return "all"
        return "none" if key0 > (c + 1) * sw - 1 else "some"

    def block(t, carry, key0=None, next_key0=None, last=False):
        out = []
        for c in range(len(strips)):
            m, bm = carry[c]
            if visible(key0, c) != "none":
                m = attend(t, c, m, bm, key0 if visible(key0, c) == "some" else None)
            if not last and visible(next_key0, c) != "none":
                bm = scores(t + 1, c)
            out.append((m, bm))
        return tuple(out)

    acc_ref[...] = jnp.zeros_like(acc_ref)
    init = tuple((jnp.full((1, sw), MASK_VALUE, F32), scores(0, c)) for c in range(len(strips)))

    def blocks(first, count, carry):
        for j in range(count):
            carry = block(first + j, carry)
        return carry

    carry = lax.fori_loop(0, qi // 2, lambda w, carry: blocks(4 * w, 4, carry), init)
    carry = lax.cond(qi % 2 == 1, lambda carry: blocks(2 * qi - 2, 2, carry), lambda carry: carry, carry)

    carry = block(2 * qi, carry, key0=0, next_key0=tk)
    block(2 * qi + 1, carry, key0=tk, last=True)
    inv_l = 1.0 / acc_ref[HEAD_DIM:HEAD_DIM + 1, :]
    o_ref[...] = (acc_ref[0:HEAD_DIM, :] * inv_l).T.astype(o_ref.dtype)


def _flash(r, qt, k, vt):
    n_heads, depth, s = qt.shape
    nb, tk = k.shape[1], k.shape[2]
    tq = Q_BLOCK
    assert tq == 2 * tk and s % tq == 0
    v_rows = vt.shape[2]
    vmem = (2 * nb * tk * (depth + v_rows) * 2 + 2 * depth * tq * 2 + 2 * tq * HEAD_DIM * 2
            + 2 * tk * tq * 6 + v_rows * tq * 4 + 3 * tk * tq * 4 + 4 * MIB)
    return pl.pallas_call(
        _flash_kernel,
        grid=(n_heads, s // tq),
        in_specs=[pl.BlockSpec(memory_space=pltpu.SMEM),
                  pl.BlockSpec((None, depth, tq), lambda h, i: (h, 0, i)),
                  pl.BlockSpec((None, nb, tk, depth), lambda h, i: (h, 0, 0, 0)),
                  pl.BlockSpec((None, nb, v_rows, tk), lambda h, i: (h, 0, 0, 0))],
        out_specs=pl.BlockSpec((tq, HEAD_DIM), lambda h, i: (i, h)),
        out_shape=jax.ShapeDtypeStruct((s, n_heads * HEAD_DIM), BF16),
        scratch_shapes=[pltpu.VMEM((tk, tq), F32),
                        pltpu.VMEM((v_rows, tq), F32)],
        compiler_params=_params(("arbitrary", "arbitrary"), vmem),
        name="flash",
    )(r, qt, k, vt)


def _mix_kernel(x_ref, gc_ref, ga_ref, cb_ref, cc_ref, cv_ref, ccp_ref, cvp_ref, at_ref,
                cw_ref, bg_ref, wco_ref, wao_ref, wmo_ref, o_ref, ext_ref):
    i = pl.program_id(0)
    tm = x_ref.shape[0]
    halo = ccp_ref.shape[0]
    d = x_ref.shape[1]

    prev = ccp_ref[...].astype(F32) * cvp_ref[...].astype(F32)
    ext_ref[0:halo, :] = jnp.where(i == 0, 0.0, prev)
    ext_ref[halo:, :] = cc_ref[...].astype(F32) * cv_ref[...].astype(F32)
    y = None
    for tap in range(CONV_K):
        shifted = ext_ref[pl.ds(halo - (CONV_K - 1) + tap, tm), :]
        term = cw_ref[tap:tap + 1, :] * shifted
        y = term if y is None else y + term
    conv_y = (cb_ref[...].astype(F32) * y).astype(BF16)

    conv_branch = jnp.dot(conv_y, wco_ref[...], preferred_element_type=F32)
    attn_branch = jnp.dot(at_ref[...], wao_ref[...], preferred_element_type=F32)
    gate_c = jax.nn.sigmoid(gc_ref[...].astype(F32) + bg_ref[:, 0:d])
    gate_a = jax.nn.sigmoid(ga_ref[...].astype(F32) + bg_ref[:, d:])
    merged = (gate_c * conv_branch + gate_a * attn_branch).astype(BF16)
    o_ref[...] = x_ref[...] + jnp.dot(merged, wmo_ref[...], preferred_element_type=F32)


def _mix(x, z, attn, conv_w, b_gate, wco, wao, wmo, layer):
    s, d = x.shape
    c = wco.shape[1]
    a = wao.shape[1]
    assert d == 2 * c and z.shape[1] == 2 * d + 3 * c
    tm = MIX_TM
    halo = V7X_BF16_SUBLANES
    assert s % tm == 0 and tm % halo == 0 and halo >= CONV_K - 1
    rows_per_tile = tm // halo
    prev_rows = lambda i: jnp.maximum(i * rows_per_tile - 1, 0)
    vmem = (4 * tm * d * 4 + 4 * tm * d * 2 + 6 * tm * c * 2 + 2 * tm * a * 2
            + (c + a + d) * d * 2 + (tm + halo) * c * 4
            + 4 * tm * d * 4 + 3 * tm * c * 4 + 4 * MIB)
    return pl.pallas_call(
        _mix_kernel,
        grid=(s // tm,),
        in_specs=[pl.BlockSpec((tm, d), lambda i: (i, 0)),
                  pl.BlockSpec((tm, d), lambda i: (i, 0)),
                  pl.BlockSpec((tm, d), lambda i: (i, 1)),
                  pl.BlockSpec((tm, c), lambda i: (i, 4)),
                  pl.BlockSpec((tm, c), lambda i: (i, 5)),
                  pl.BlockSpec((tm, c), lambda i: (i, 6)),
                  pl.BlockSpec((halo, c), lambda i: (prev_rows(i), 5)),
                  pl.BlockSpec((halo, c), lambda i: (prev_rows(i), 6)),
                  pl.BlockSpec((tm, a), lambda i: (i, 0)),
                  _resident((CONV_K, c), lambda i: (0, 0)),
                  _resident((1, 2 * d), lambda i: (0, 0)),
                  _resident((None, c, d), lambda i: (layer, 0, 0)),
                  _resident((None, a, d), lambda i: (layer, 0, 0)),
                  _resident((None, d, d), lambda i: (layer, 0, 0))],
        out_specs=pl.BlockSpec((tm, d), lambda i: (i, 0)),
        out_shape=jax.ShapeDtypeStruct((s, d), F32),
        scratch_shapes=[pltpu.VMEM((tm + halo, c), F32)],
        compiler_params=_params(("arbitrary",), vmem),
        name="mix",
    )(x, z, z, z, z, z, z, z, attn, conv_w, b_gate, wco, wao, wmo)


def _mlp_kernel(x_ref, g_ref, w1_ref, w2_ref, gf_ref, o_ref, h_ref, *, final_norm):
    j = pl.program_id(1)

    @pl.when(j == 0)
    def _():
        x = x_ref[...]
        h_ref[...] = _rmsnorm(x, g_ref[...]).astype(BF16)
        o_ref[...] = x

    a = jnp.dot(h_ref[...], w1_ref[...], preferred_element_type=F32)
    u = jnp.square(jnp.maximum(a, 0.0)).astype(BF16)
    o_ref[...] += jnp.dot(u, w2_ref[...], preferred_element_type=F32)

    if final_norm:
        @pl.when(j == pl.num_programs(1) - 1)
        def _():
            o_ref[...] = _rmsnorm(o_ref[...], gf_ref[...])


def _mlp(x, g, w1, w2, g_final, final_norm):
    s, d = x.shape
    ff = w1.shape[1]
    tm, tf = MLP_TM, MLP_TF
    assert s % tm == 0 and ff % tf == 0
    vmem = (4 * tm * d * 4 + tm * d * 2 + 4 * d * tf * 2
            + tm * tf * 6 + 2 * tm * d * 4 + 4 * MIB)
    return pl.pallas_call(
        functools.partial(_mlp_kernel, final_norm=final_norm),
        grid=(s // tm, ff // tf),
        in_specs=[pl.BlockSpec((tm, d), lambda i, j: (i, 0)),
                  pl.BlockSpec((1, d), lambda i, j: (0, 0)),
                  pl.BlockSpec((d, tf), lambda i, j: (0, j)),
                  pl.BlockSpec((tf, d), lambda i, j: (j, 0)),
                  pl.BlockSpec((1, d), lambda i, j: (0, 0))],
        out_specs=pl.BlockSpec((tm, d), lambda i, j: (i, 0)),
        out_shape=jax.ShapeDtypeStruct((s, d), F32),
        scratch_shapes=[pltpu.VMEM((tm, d), BF16)],
        compiler_params=_params(("arbitrary", "arbitrary"), vmem),
        name="mlp",
    )(x, g, w1, w2, g_final)


def kernel(x, g_mix, w_in, b_f, b_gate, conv_w, w_conv_out, w_attn_out, w_mix_out, g_mlp, w_ff1, w_ff2, g_final):
    b, s, d = x.shape
    depth = g_mix.shape[0]
    c = w_conv_out.shape[1]
    a = w_attn_out.shape[1]
    n_heads = b_f.shape[1]
    assert a == n_heads * HEAD_DIM and w_in.shape[2] == 3 * c + 3 * a + n_heads + 2 * d
    o_q, o_f, o_g = 3 * c, 3 * c + 3 * a, 3 * c + 3 * a + n_heads

    wqt, wk, wvt, wf = _qkvf_weights(w_in, o_q, o_f, a)
    wco, wao, wmo = _cast_bf16(w_conv_out), _cast_bf16(w_attn_out), _cast_bf16(w_mix_out)
    w_gate = w_in[:, :, o_g:].astype(BF16)
    w_conv = _cast_bf16(w_in, width=o_q)

    outs = []
    for bi in range(b):
        xs = x[bi]
        for l in range(depth):
            bf = jnp.pad(b_f[l], (0, V7X_LANES - n_heads)).reshape(1, V7X_LANES)
            g1 = g_mix[l].reshape(1, d)

            z = _inproj(xs, g1, w_gate, w_conv, l)
            qt, k, vt, r = _attn_prep(xs, g1, wqt, wk, wvt, wf, bf, l)
            r_heads = r[:, 0, :n_heads].T
            attn = _flash(r_heads, qt, k, vt)
            xs = _mix(xs, z, attn, conv_w[l], b_gate[l].reshape(1, 2 * d), wco, wao, wmo, l)
            xs = _mlp(xs, g_mlp[l].reshape(1, d), w_ff1[l].astype(BF16), w_ff2[l].astype(BF16),
                      g_final.reshape(1, d), final_norm=(l == depth - 1))
        outs.append(xs)
    return jnp.stack(outs, axis=0)
```

```python
import functools

import jax
import jax.numpy as jnp
from jax import lax
from jax.experimental import pallas as pl
from jax.experimental.pallas import tpu as pltpu

EPS = 1e-6
HEAD_DIM = 128
CONV_K = 3

V7X_LANES = 128
V7X_MXU_DEPTH = 256
V7X_F32_SUBLANES = 8
V7X_BF16_SUBLANES = 16
V7X_VMEM_BYTES = 64 * 1024 * 1024
MIB = 1024 * 1024

INPROJ_TM = 1024
INPROJ_TN = 1024
KV_BLOCK = 512
Q_BLOCK = 2 * KV_BLOCK
MIX_TM = 256
MLP_TM = 1024
MLP_TF = 512
SUM_ROWS = V7X_BF16_SUBLANES
V_ROWS = HEAD_DIM + SUM_ROWS

MASK_VALUE = -1e30
LOG2_E = 1.4426950408889634

BF16 = jnp.bfloat16
F32 = jnp.float32


def _rmsnorm(x, g):
    ms = jnp.mean(x * x, axis=-1, keepdims=True)
    return x * lax.rsqrt(ms + EPS) * g


def _split3_bf16(v):
    hi = v.astype(BF16).astype(F32)
    r1 = v - hi
    mid = r1.astype(BF16).astype(F32)
    lo = (r1 - mid).astype(BF16).astype(F32)
    return hi, mid, lo


def _params(semantics, vmem_bytes, flags=None):
    assert vmem_bytes <= V7X_VMEM_BYTES - 4 * MIB, vmem_bytes
    return pltpu.CompilerParams(dimension_semantics=semantics, vmem_limit_bytes=int(vmem_bytes), flags=flags)


def _resident(block_shape, index_map):
    return pl.BlockSpec(block_shape, index_map, pipeline_mode=pl.Buffered(1))


def _inproj_kernel(x_ref, g_ref, w_ref, z_ref, h_ref):
    @pl.when(pl.program_id(1) == 0)
    def _():
        h_ref[...] = _rmsnorm(x_ref[...], g_ref[...]).astype(BF16)

    z_ref[...] = jnp.dot(h_ref[...], w_ref[...], preferred_element_type=F32).astype(z_ref.dtype)


def _inproj(x, g, w):
    s, d = x.shape
    n = w.shape[1]
    tm, tn = min(INPROJ_TM, s), INPROJ_TN
    assert s % tm == 0 and n % tn == 0
    vmem = (2 * tm * d * 4 + tm * d * 2 + 2 * d * tn * 2 + 2 * tm * tn * 2
            + tm * tn * 4 + tm * d * 4 + 4 * MIB)
    return pl.pallas_call(
        _inproj_kernel,
        grid=(s // tm, n // tn),
        in_specs=[pl.BlockSpec((tm, d), lambda i, j: (i, 0)),
                  pl.BlockSpec((1, d), lambda i, j: (0, 0)),
                  pl.BlockSpec((d, tn), lambda i, j: (0, j))],
        out_specs=pl.BlockSpec((tm, tn), lambda i, j: (i, j)),
        out_shape=jax.ShapeDtypeStruct((s, n), BF16),
        scratch_shapes=[pltpu.VMEM((tm, d), BF16)],
        compiler_params=_params(("arbitrary", "arbitrary"), vmem),
        name="inproj",
    )(x, g, w)


def _prep_kernel(x_ref, g_ref, wqt_ref, wk_ref, wvt_ref, wf_ref, bf_ref,
                 qt_ref, k_ref, vt_ref, r_ref, carry_ref, *, n_heads, scale):
    i = pl.program_id(0)
    tm = x_ref.shape[0]

    @pl.when(i == 0)
    def _():
        carry_ref[...] = jnp.zeros_like(carry_ref)

    h = _rmsnorm(x_ref[...], g_ref[...]).astype(BF16)
    nt = (((1,), (1,)), ((), ()))
    qt = lax.dot_general(wqt_ref[...], h, nt, preferred_element_type=F32) * scale
    vt = lax.dot_general(wvt_ref[...], h, nt, preferred_element_type=F32)
    k = jnp.dot(h, wk_ref[...], preferred_element_type=F32)
    f = jnp.dot(h, wf_ref[...], preferred_element_type=F32) + bf_ref[...]

    lane = lax.broadcasted_iota(jnp.int32, f.shape, 1)
    logf = jnp.where(lane < n_heads, jax.nn.log_sigmoid(f) * LOG2_E, 0.0)

    row = lax.broadcasted_iota(jnp.int32, (tm, tm), 0)
    col = lax.broadcasted_iota(jnp.int32, (tm, tm), 1)
    tril = (col <= row).astype(BF16)
    c_rel = None
    for part in _split3_bf16(logf):
        term = jnp.dot(tril, part.astype(BF16), preferred_element_type=F32)
        c_rel = term if c_rel is None else c_rel + term

    carry = carry_ref[...]
    r_ref[0] = carry
    carry_ref[...] = carry + c_rel[tm - 1:tm, :]

    nhi, nmid, nlo = _split3_bf16(-c_rel)
    bias_cols = (nhi + pltpu.roll(nmid, n_heads, 1) + pltpu.roll(nlo, 2 * n_heads, 1)).astype(BF16)

    sel_row = lax.broadcasted_iota(jnp.int32, (V7X_LANES, tm), 0)
    sum_rows = (lax.broadcasted_iota(jnp.int32, (SUM_ROWS, tm), 0) == 0).astype(BF16)
    for hh in range(n_heads):
        sl = slice(hh * HEAD_DIM, (hh + 1) * HEAD_DIM)
        onehot = ((sel_row == hh) | (sel_row == n_heads + hh) | (sel_row == 2 * n_heads + hh))
        qt_ref[hh, 0:HEAD_DIM, :] = qt[sl, :].astype(BF16)
        qt_ref[hh, HEAD_DIM:, :] = onehot.astype(BF16)
        k_ref[hh, 0, :, 0:HEAD_DIM] = k[:, sl].astype(BF16)
        k_ref[hh, 0, :, HEAD_DIM:] = bias_cols
        vt_ref[hh, 0, 0:HEAD_DIM, :] = vt[sl, :].astype(BF16)
        vt_ref[hh, 0, HEAD_DIM:, :] = sum_rows


def _attn_prep(x, g, wqt, wk, wvt, wf, bf):
    s, d = x.shape
    a = wk.shape[1]
    n_heads = a // HEAD_DIM
    assert 3 * n_heads <= V7X_LANES and 2 * HEAD_DIM == V7X_MXU_DEPTH
    tm = KV_BLOCK
    assert s % tm == 0
    nb = s // tm
    vmem = (2 * tm * d * 4 + 3 * d * a * 2 + d * V7X_LANES * 2
            + 2 * n_heads * tm * (2 * V7X_MXU_DEPTH + V_ROWS) * 2
            + tm * d * 6 + 3 * tm * a * 4 + tm * tm * 2 + 6 * MIB)
    kern = functools.partial(_prep_kernel, n_heads=n_heads, scale=LOG2_E * float(HEAD_DIM) ** -0.5)
    return pl.pallas_call(
        kern,
        grid=(nb,),
        in_specs=[pl.BlockSpec((tm, d), lambda i: (i, 0)),
                  _resident((1, d), lambda i: (0, 0)),
                  _resident((a, d), lambda i: (0, 0)),
                  _resident((d, a), lambda i: (0, 0)),
                  _resident((a, d), lambda i: (0, 0)),
                  _resident((d, V7X_LANES), lambda i: (0, 0)),
                  _resident((1, V7X_LANES), lambda i: (0, 0))],
        out_specs=[pl.BlockSpec((n_heads, V7X_MXU_DEPTH, tm), lambda i: (0, 0, i)),
                   pl.BlockSpec((n_heads, 1, tm, V7X_MXU_DEPTH), lambda i: (0, i, 0, 0)),
                   pl.BlockSpec((n_heads, 1, V_ROWS, tm), lambda i: (0, i, 0, 0)),
                   pl.BlockSpec((1, V7X_F32_SUBLANES, V7X_LANES), lambda i: (i, 0, 0))],
        out_shape=[jax.ShapeDtypeStruct((n_heads, V7X_MXU_DEPTH, s), BF16),
                   jax.ShapeDtypeStruct((n_heads, nb, tm, V7X_MXU_DEPTH), BF16),
                   jax.ShapeDtypeStruct((n_heads, nb, V_ROWS, tm), BF16),
                   jax.ShapeDtypeStruct((nb, V7X_F32_SUBLANES, V7X_LANES), F32)],
        scratch_shapes=[pltpu.VMEM((V7X_F32_SUBLANES, V7X_LANES), F32)],
        compiler_params=_params(("arbitrary",), vmem),
        name="attn_prep",
    )(x, g, wqt, wk, wvt, wf, bf)


def _flash_kernel(r_ref, qt_ref, k_ref, vt_ref, o_ref, s_ref, acc_ref):
    hd = pl.program_id(0)
    qi = pl.program_id(1)
    tk = k_ref.shape[1]
    tq = qt_ref.shape[1]

    sw = V7X_MXU_DEPTH
    strips = [slice(c * sw, (c + 1) * sw) for c in range(tq // sw)]

    def scores(blk, c):
        s = jnp.dot(k_ref[blk], qt_ref[:, strips[c]], preferred_element_type=F32)
        s_ref[:, strips[c]] = s
        return jnp.max(s, axis=0, keepdims=True)

    def attend(blk, c, m, bm, key0=None):
        s = s_ref[:, strips[c]]
        if key0 is not None:
            key = key0 + lax.broadcasted_iota(jnp.int32, (tk, sw), 0)
            qry = c * sw + lax.broadcasted_iota(jnp.int32, (tk, sw), 1)
            s = jnp.where(key <= qry, s, MASK_VALUE)
            bm = jnp.max(s, axis=0, keepdims=True)
        rb = r_ref[hd, blk]
        m_new = jnp.maximum(m, bm - rb)
        alpha = jnp.exp2(m - m_new)
        p = jnp.exp2(s - (m_new + rb)).astype(BF16)
        acc_ref[:, strips[c]] = alpha * acc_ref[:, strips[c]] + jnp.dot(
            vt_ref[blk], p, preferred_element_type=F32)
        return m_new

    def visible(key0, c):
        if key0 is None or key0 + tk - 1 <= c * sw:
            return "all"
        return "none" if key0 > (c + 1) * sw - 1 else "some"

    def block(t, carry, key0=None, next_key0=None, last=False):
        out = []
        for c in range(len(strips)):
            m, bm = carry[c]
            if visible(key0, c) != "none":
                m = attend(t, c, m, bm, key0 if visible(key0, c) == "some" else None)
            if not last and visible(next_key0, c) != "none":
                bm = scores(t + 1, c)
            out.append((m, bm))
        return tuple(out)

    acc_ref[...] = jnp.zeros_like(acc_ref)
    init = tuple((jnp.full((1, sw), MASK_VALUE, F32), scores(0, c)) for c in range(len(strips)))

    def blocks(first, count, carry):
        for j in range(count):
            carry = block(first + j, carry)
        return carry

    done = 8 * (qi // 4)
    carry = lax.fori_loop(0, qi // 4, lambda w, carry: blocks(8 * w, 8, carry), init)
    carry = lax.cond(qi % 4 >= 2, lambda carry: blocks(done, 4, carry), lambda carry: carry, carry)
    carry = lax.cond(qi % 2 == 1, lambda carry: blocks(2 * qi - 2, 2, carry), lambda carry: carry, carry)

    carry = block(2 * qi, carry, key0=0, next_key0=tk)
    block(2 * qi + 1, carry, key0=tk, last=True)
    inv_l = 1.0 / acc_ref[HEAD_DIM:HEAD_DIM + 1, :]
    o_ref[...] = (acc_ref[0:HEAD_DIM, :] * inv_l).T.astype(o_ref.dtype)


def _flash(r, qt, k, vt):
    n_heads, depth, s = qt.shape
    nb, tk = k.shape[1], k.shape[2]
    tq = Q_BLOCK
    assert tq == 2 * tk and s % tq == 0
    v_rows = vt.shape[2]
    vmem = (2 * nb * tk * (depth + v_rows) * 2 + 2 * depth * tq * 2 + 2 * tq * HEAD_DIM * 2
            + 2 * tk * tq * 6 + v_rows * tq * 4 + 3 * tk * tq * 4 + 4 * MIB)
    return pl.pallas_call(
        _flash_kernel,
        grid=(n_heads, s // tq),
        in_specs=[pl.BlockSpec(memory_space=pltpu.SMEM),
                  pl.BlockSpec((None, depth, tq), lambda h, i: (h, 0, i)),
                  pl.BlockSpec((None, nb, tk, depth), lambda h, i: (h, 0, 0, 0)),
                  pl.BlockSpec((None, nb, v_rows, tk), lambda h, i: (h, 0, 0, 0))],
        out_specs=pl.BlockSpec((tq, HEAD_DIM), lambda h, i: (i, h)),
        out_shape=jax.ShapeDtypeStruct((s, n_heads * HEAD_DIM), BF16),
        scratch_shapes=[pltpu.VMEM((tk, tq), F32),
                        pltpu.VMEM((v_rows, tq), F32)],
        compiler_params=_params(("arbitrary", "arbitrary"), vmem),
        name="flash",
    )(r, qt, k, vt)


def _mix_kernel(x_ref, gc_ref, ga_ref, cb_ref, cc_ref, cv_ref, ccp_ref, cvp_ref, at_ref,
                cw_ref, bg_ref, wco_ref, wao_ref, wmo_ref, o_ref, ext_ref):
    i = pl.program_id(0)
    tm = x_ref.shape[0]
    halo = ccp_ref.shape[0]
    d = x_ref.shape[1]

    prev = ccp_ref[...].astype(F32) * cvp_ref[...].astype(F32)
    ext_ref[0:halo, :] = jnp.where(i == 0, 0.0, prev)
    ext_ref[halo:, :] = cc_ref[...].astype(F32) * cv_ref[...].astype(F32)
    y = None
    for tap in range(CONV_K):
        shifted = ext_ref[pl.ds(halo - (CONV_K - 1) + tap, tm), :]
        term = cw_ref[tap:tap + 1, :] * shifted
        y = term if y is None else y + term
    conv_y = (cb_ref[...].astype(F32) * y).astype(BF16)

    conv_branch = jnp.dot(conv_y, wco_ref[...], preferred_element_type=F32)
    attn_branch = jnp.dot(at_ref[...], wao_ref[...], preferred_element_type=F32)
    gate_c = jax.nn.sigmoid(gc_ref[...].astype(F32) + bg_ref[:, 0:d])
    gate_a = jax.nn.sigmoid(ga_ref[...].astype(F32) + bg_ref[:, d:])
    merged = (gate_c * conv_branch + gate_a * attn_branch).astype(BF16)
    o_ref[...] = x_ref[...] + jnp.dot(merged, wmo_ref[...], preferred_element_type=F32)


def _mix(x, z, attn, conv_w, b_gate, wco, wao, wmo):
    s, d = x.shape
    c = wco.shape[0]
    a = wao.shape[0]
    assert d == 2 * c and z.shape[1] == 2 * d + 3 * c
    tm = MIX_TM
    halo = V7X_BF16_SUBLANES
    assert s % tm == 0 and tm % halo == 0 and halo >= CONV_K - 1
    rows_per_tile = tm // halo
    prev_rows = lambda i: jnp.maximum(i * rows_per_tile - 1, 0)
    vmem = (4 * tm * d * 4 + 4 * tm * d * 2 + 6 * tm * c * 2 + 2 * tm * a * 2
            + (c + a + d) * d * 2 + (tm + halo) * c * 4
            + 4 * tm * d * 4 + 3 * tm * c * 4 + 4 * MIB)
    return pl.pallas_call(
        _mix_kernel,
        grid=(s // tm,),
        in_specs=[pl.BlockSpec((tm, d), lambda i: (i, 0)),
                  pl.BlockSpec((tm, d), lambda i: (i, 0)),
                  pl.BlockSpec((tm, d), lambda i: (i, 1)),
                  pl.BlockSpec((tm, c), lambda i: (i, 4)),
                  pl.BlockSpec((tm, c), lambda i: (i, 5)),
                  pl.BlockSpec((tm, c), lambda i: (i, 6)),
                  pl.BlockSpec((halo, c), lambda i: (prev_rows(i), 5)),
                  pl.BlockSpec((halo, c), lambda i: (prev_rows(i), 6)),
                  pl.BlockSpec((tm, a), lambda i: (i, 0)),
                  _resident((CONV_K, c), lambda i: (0, 0)),
                  _resident((1, 2 * d), lambda i: (0, 0)),
                  _resident((c, d), lambda i: (0, 0)),
                  _resident((a, d), lambda i: (0, 0)),
                  _resident((d, d), lambda i: (0, 0))],
        out_specs=pl.BlockSpec((tm, d), lambda i: (i, 0)),
        out_shape=jax.ShapeDtypeStruct((s, d), F32),
        scratch_shapes=[pltpu.VMEM((tm + halo, c), F32)],
        compiler_params=_params(("arbitrary",), vmem),
        name="mix",
    )(x, z, z, z, z, z, z, z, attn, conv_w, b_gate, wco, wao, wmo)


def _mlp_kernel(x_ref, g_ref, w1_ref, w2_ref, gf_ref, o_ref, h_ref, *, final_norm):
    j = pl.program_id(1)

    @pl.when(j == 0)
    def _():
        x = x_ref[...]
        h_ref[...] = _rmsnorm(x, g_ref[...]).astype(BF16)
        o_ref[...] = x

    a = jnp.dot(h_ref[...], w1_ref[...], preferred_element_type=F32)
    u = jnp.square(jnp.maximum(a, 0.0)).astype(BF16)
    o_ref[...] += jnp.dot(u, w2_ref[...], preferred_element_type=F32)

    if final_norm:
        @pl.when(j == pl.num_programs(1) - 1)
        def _():
            o_ref[...] = _rmsnorm(o_ref[...], gf_ref[...])


def _mlp(x, g, w1, w2, g_final, final_norm):
    s, d = x.shape
    ff = w1.shape[1]
    tm, tf = MLP_TM, MLP_TF
    assert s % tm == 0 and ff % tf == 0
    vmem = (4 * tm * d * 4 + tm * d * 2 + 4 * d * tf * 2
            + tm * tf * 6 + tm * d * 4 + 2 * MIB)
    return pl.pallas_call(
        functools.partial(_mlp_kernel, final_norm=final_norm),
        grid=(s // tm, ff // tf),
        in_specs=[pl.BlockSpec((tm, d), lambda i, j: (i, 0)),
                  pl.BlockSpec((1, d), lambda i, j: (0, 0)),
                  pl.BlockSpec((d, tf), lambda i, j: (0, j)),
                  pl.BlockSpec((tf, d), lambda i, j: (j, 0)),
                  pl.BlockSpec((1, d), lambda i, j: (0, 0))],
        out_specs=pl.BlockSpec((tm, d), lambda i, j: (i, 0)),
        out_shape=jax.ShapeDtypeStruct((s, d), F32),
        scratch_shapes=[pltpu.VMEM((tm, d), BF16)],
        compiler_params=_params(("arbitrary", "arbitrary"), vmem),
        name="mlp",
    )(x, g, w1, w2, g_final)


def kernel(x, g_mix, w_in, b_f, b_gate, conv_w, w_conv_out, w_attn_out, w_mix_out, g_mlp, w_ff1, w_ff2, g_final):
    b, s, d = x.shape
    depth = g_mix.shape[0]
    c = w_conv_out.shape[1]
    a = w_attn_out.shape[1]
    n_heads = b_f.shape[1]
    assert a == n_heads * HEAD_DIM and w_in.shape[2] == 3 * c + 3 * a + n_heads + 2 * d
    o_q, o_k, o_v, o_f, o_g = 3 * c, 3 * c + a, 3 * c + 2 * a, 3 * c + 3 * a, 3 * c + 3 * a + n_heads

    outs = []
    for bi in range(b):
        xs = x[bi]
        for l in range(depth):
            w = w_in[l]
            w_cg = jnp.concatenate([w[:, o_g:], w[:, :o_q]], axis=1).astype(BF16)
            wqt = w[:, o_q:o_k].T.astype(BF16)
            wk = w[:, o_k:o_v].astype(BF16)
            wvt = w[:, o_v:o_f].T.astype(BF16)
            wf = jnp.pad(w[:, o_f:o_g], ((0, 0), (0, V7X_LANES - n_heads))).astype(BF16)
            bf = jnp.pad(b_f[l], (0, V7X_LANES - n_heads)).reshape(1, V7X_LANES)
            g1 = g_mix[l].reshape(1, d)

            z = _inproj(xs, g1, w_cg)
            qt, k, vt, r = _attn_prep(xs, g1, wqt, wk, wvt, wf, bf)
            r_heads = r[:, 0, :n_heads].T
            attn = _flash(r_heads, qt, k, vt)
            xs = _mix(xs, z, attn, conv_w[l], b_gate[l].reshape(1, 2 * d),
                      w_conv_out[l].astype(BF16), w_attn_out[l].astype(BF16), w_mix_out[l].astype(BF16))
            xs = _mlp(xs, g_mlp[l].reshape(1, d), w_ff1[l].astype(BF16), w_ff2[l].astype(BF16),
                      g_final.reshape(1, d), final_norm=(l == depth - 1))
        outs.append(xs)
    return jnp.stack(outs, axis=0)
```

```python
import functools

import jax
import jax.numpy as jnp
from jax import lax
from jax.experimental import pallas as pl
from jax.experimental.pallas import tpu as pltpu

EPS = 1e-6
HEAD_DIM = 128
CONV_K = 3

V7X_LANES = 128
V7X_MXU_DEPTH = 256
V7X_F32_SUBLANES = 8
V7X_BF16_SUBLANES = 16
V7X_VMEM_BYTES = 64 * 1024 * 1024
MIB = 1024 * 1024

INPROJ_TM = 1024
INPROJ_TN = 1024
KV_BLOCK = 512
Q_BLOCK = 2 * KV_BLOCK
MIX_TM = 256
MLP_TM = 512
MLP_TF = 1024
SUM_ROWS = V7X_BF16_SUBLANES
V_ROWS = HEAD_DIM + SUM_ROWS

MASK_VALUE = -1e30
LOG2_E = 1.4426950408889634

BF16 = jnp.bfloat16
F32 = jnp.float32


def _rmsnorm(x, g):
    ms = jnp.mean(x * x, axis=-1, keepdims=True)
    return x * lax.rsqrt(ms + EPS) * g


def _split3_bf16(v):
    hi = v.astype(BF16).astype(F32)
    r1 = v - hi
    mid = r1.astype(BF16).astype(F32)
    lo = (r1 - mid).astype(BF16).astype(F32)
    return hi, mid, lo


def _params(semantics, vmem_bytes, flags=None):
    assert vmem_bytes <= V7X_VMEM_BYTES - 4 * MIB, vmem_bytes
    return pltpu.CompilerParams(dimension_semantics=semantics, vmem_limit_bytes=int(vmem_bytes), flags=flags)


def _resident(block_shape, index_map):
    return pl.BlockSpec(block_shape, index_map, pipeline_mode=pl.Buffered(1))


def _inproj_kernel(x_ref, g_ref, w_ref, z_ref, h_ref):
    @pl.when(pl.program_id(1) == 0)
    def _():
        h_ref[...] = _rmsnorm(x_ref[...], g_ref[...]).astype(BF16)

    z_ref[...] = jnp.dot(h_ref[...], w_ref[...], preferred_element_type=F32).astype(z_ref.dtype)


def _inproj(x, g, w):
    s, d = x.shape
    n = w.shape[1]
    tm, tn = min(INPROJ_TM, s), INPROJ_TN
    assert s % tm == 0 and n % tn == 0
    vmem = (2 * tm * d * 4 + tm * d * 2 + 2 * d * tn * 2 + 2 * tm * tn * 2
            + tm * tn * 4 + tm * d * 4 + 4 * MIB)
    return pl.pallas_call(
        _inproj_kernel,
        grid=(s // tm, n // tn),
        in_specs=[pl.BlockSpec((tm, d), lambda i, j: (i, 0)),
                  pl.BlockSpec((1, d), lambda i, j: (0, 0)),
                  pl.BlockSpec((d, tn), lambda i, j: (0, j))],
        out_specs=pl.BlockSpec((tm, tn), lambda i, j: (i, j)),
        out_shape=jax.ShapeDtypeStruct((s, n), BF16),
        scratch_shapes=[pltpu.VMEM((tm, d), BF16)],
        compiler_params=_params(("arbitrary", "arbitrary"), vmem),
        name="inproj",
    )(x, g, w)


def _prep_kernel(x_ref, g_ref, wqt_ref, wk_ref, wvt_ref, wf_ref, bf_ref,
                 qt_ref, k_ref, vt_ref, r_ref, carry_ref, *, n_heads, scale):
    i = pl.program_id(0)
    tm = x_ref.shape[0]

    @pl.when(i == 0)
    def _():
        carry_ref[...] = jnp.zeros_like(carry_ref)

    h = _rmsnorm(x_ref[...], g_ref[...]).astype(BF16)
    nt = (((1,), (1,)), ((), ()))
    qt = lax.dot_general(wqt_ref[...], h, nt, preferred_element_type=F32) * scale
    vt = lax.dot_general(wvt_ref[...], h, nt, preferred_element_type=F32)
    k = jnp.dot(h, wk_ref[...], preferred_element_type=F32)
    f = jnp.dot(h, wf_ref[...], preferred_element_type=F32) + bf_ref[...]

    lane = lax.broadcasted_iota(jnp.int32, f.shape, 1)
    logf = jnp.where(lane < n_heads, jax.nn.log_sigmoid(f) * LOG2_E, 0.0)

    row = lax.broadcasted_iota(jnp.int32, (tm, tm), 0)
    col = lax.broadcasted_iota(jnp.int32, (tm, tm), 1)
    tril = (col <= row).astype(BF16)
    c_rel = None
    for part in _split3_bf16(logf):
        term = jnp.dot(tril, part.astype(BF16), preferred_element_type=F32)
        c_rel = term if c_rel is None else c_rel + term

    carry = carry_ref[...]
    r_ref[0] = carry
    carry_ref[...] = carry + c_rel[tm - 1:tm, :]

    nhi, nmid, nlo = _split3_bf16(-c_rel)
    bias_cols = (nhi + pltpu.roll(nmid, n_heads, 1) + pltpu.roll(nlo, 2 * n_heads, 1)).astype(BF16)

    sel_row = lax.broadcasted_iota(jnp.int32, (V7X_LANES, tm), 0)
    sum_rows = (lax.broadcasted_iota(jnp.int32, (SUM_ROWS, tm), 0) == 0).astype(BF16)
    for hh in range(n_heads):
        sl = slice(hh * HEAD_DIM, (hh + 1) * HEAD_DIM)
        onehot = ((sel_row == hh) | (sel_row == n_heads + hh) | (sel_row == 2 * n_heads + hh))
        qt_ref[hh, 0:HEAD_DIM, :] = qt[sl, :].astype(BF16)
        qt_ref[hh, HEAD_DIM:, :] = onehot.astype(BF16)
        k_ref[hh, 0, :, 0:HEAD_DIM] = k[:, sl].astype(BF16)
        k_ref[hh, 0, :, HEAD_DIM:] = bias_cols
        vt_ref[hh, 0, 0:HEAD_DIM, :] = vt[sl, :].astype(BF16)
        vt_ref[hh, 0, HEAD_DIM:, :] = sum_rows


def _attn_prep(x, g, wqt, wk, wvt, wf, bf):
    s, d = x.shape
    a = wk.shape[1]
    n_heads = a // HEAD_DIM
    assert 3 * n_heads <= V7X_LANES and 2 * HEAD_DIM == V7X_MXU_DEPTH
    tm = KV_BLOCK
    assert s % tm == 0
    nb = s // tm
    vmem = (2 * tm * d * 4 + 3 * d * a * 2 + d * V7X_LANES * 2
            + 2 * n_heads * tm * (2 * V7X_MXU_DEPTH + V_ROWS) * 2
            + tm * d * 6 + 3 * tm * a * 4 + tm * tm * 2 + 6 * MIB)
    kern = functools.partial(_prep_kernel, n_heads=n_heads, scale=LOG2_E * float(HEAD_DIM) ** -0.5)
    return pl.pallas_call(
        kern,
        grid=(nb,),
        in_specs=[pl.BlockSpec((tm, d), lambda i: (i, 0)),
                  _resident((1, d), lambda i: (0, 0)),
                  _resident((a, d), lambda i: (0, 0)),
                  _resident((d, a), lambda i: (0, 0)),
                  _resident((a, d), lambda i: (0, 0)),
                  _resident((d, V7X_LANES), lambda i: (0, 0)),
                  _resident((1, V7X_LANES), lambda i: (0, 0))],
        out_specs=[pl.BlockSpec((n_heads, V7X_MXU_DEPTH, tm), lambda i: (0, 0, i)),
                   pl.BlockSpec((n_heads, 1, tm, V7X_MXU_DEPTH), lambda i: (0, i, 0, 0)),
                   pl.BlockSpec((n_heads, 1, V_ROWS, tm), lambda i: (0, i, 0, 0)),
                   pl.BlockSpec((1, V7X_F32_SUBLANES, V7X_LANES), lambda i: (i, 0, 0))],
        out_shape=[jax.ShapeDtypeStruct((n_heads, V7X_MXU_DEPTH, s), BF16),
                   jax.ShapeDtypeStruct((n_heads, nb, tm, V7X_MXU_DEPTH), BF16),
                   jax.ShapeDtypeStruct((n_heads, nb, V_ROWS, tm), BF16),
                   jax.ShapeDtypeStruct((nb, V7X_F32_SUBLANES, V7X_LANES), F32)],
        scratch_shapes=[pltpu.VMEM((V7X_F32_SUBLANES, V7X_LANES), F32)],
        compiler_params=_params(("arbitrary",), vmem),
        name="attn_prep",
    )(x, g, wqt, wk, wvt, wf, bf)


def _flash_kernel(r_ref, qt_ref, qn_ref, k_ref, vt_ref, o_ref, s_ref, acc_ref, bm_ref):
    hd = pl.program_id(0)
    qi = pl.program_id(1)
    tk = k_ref.shape[1]
    tq = qt_ref.shape[1]

    sw = V7X_MXU_DEPTH
    strips = [slice(c * sw, (c + 1) * sw) for c in range(tq // sw)]

    def scores(blk, c, q_ref=qt_ref):
        s = jnp.dot(k_ref[blk], q_ref[:, strips[c]], preferred_element_type=F32)
        s_ref[:, strips[c]] = s
        return jnp.max(s, axis=0, keepdims=True)

    def attend(blk, c, m, bm, key0=None):
        s = s_ref[:, strips[c]]
        if key0 is not None:
            key = key0 + lax.broadcasted_iota(jnp.int32, (tk, sw), 0)
            qry = c * sw + lax.broadcasted_iota(jnp.int32, (tk, sw), 1)
            s = jnp.where(key <= qry, s, MASK_VALUE)
            bm = jnp.max(s, axis=0, keepdims=True)
        rb = r_ref[hd, blk]
        m_new = jnp.maximum(m, bm - rb)
        alpha = jnp.exp2(m - m_new)
        p = jnp.exp2(s - (m_new + rb)).astype(BF16)
        acc_ref[:, strips[c]] = alpha * acc_ref[:, strips[c]] + jnp.dot(
            vt_ref[blk], p, preferred_element_type=F32)
        return m_new

    def visible(key0, c):
        if key0 is None or key0 + tk - 1 <= c * sw:
            return "all"
        return "none" if key0 > (c + 1) * sw - 1 else "some"

    def block(t, carry, key0=None, next_key0=None, last=False):
        out = []
        for c in range(len(strips)):
            m, bm = carry[c]
            if visible(key0, c) != "none":
                m = attend(t, c, m, bm, key0 if visible(key0, c) == "some" else None)
            if not last and visible(next_key0, c) != "none":
                bm = scores(t + 1, c)
            out.append((m, bm))
        return tuple(out)

    acc_ref[...] = jnp.zeros_like(acc_ref)

    @pl.when(qi == 0)
    def _():
        for c in range(len(strips)):
            bm_ref[:, strips[c]] = scores(0, c)

    init = tuple((jnp.full((1, sw), MASK_VALUE, F32), bm_ref[:, strips[c]]) for c in range(len(strips)))

    def blocks(first, count, carry):
        for j in range(count):
            carry = block(first + j, carry)
        return carry

    done = 8 * (qi // 4)
    carry = lax.fori_loop(0, qi // 4, lambda w, carry: blocks(8 * w, 8, carry), init)
    carry = lax.cond(qi % 4 >= 2, lambda carry: blocks(done, 4, carry), lambda carry: carry, carry)
    carry = lax.cond(qi % 2 == 1, lambda carry: blocks(2 * qi - 2, 2, carry), lambda carry: carry, carry)

    carry = block(2 * qi, carry, key0=0, next_key0=tk)
    block(2 * qi + 1, carry, key0=tk, last=True)
    for c in range(len(strips)):
        bm_ref[:, strips[c]] = scores(0, c, qn_ref)
    inv_l = 1.0 / acc_ref[HEAD_DIM:HEAD_DIM + 1, :]
    o_ref[...] = (acc_ref[0:HEAD_DIM, :] * inv_l).T.astype(o_ref.dtype)


def _flash(r, qt, k, vt):
    n_heads, depth, s = qt.shape
    nb, tk = k.shape[1], k.shape[2]
    tq = Q_BLOCK
    assert tq == 2 * tk and s % tq == 0
    v_rows = vt.shape[2]
    n_q = s // tq
    vmem = (2 * nb * tk * (depth + v_rows) * 2 + 4 * depth * tq * 2 + 2 * tq * HEAD_DIM * 2
            + 2 * tk * tq * 6 + v_rows * tq * 4 + 3 * tk * tq * 4 + 4 * MIB)
    return pl.pallas_call(
        _flash_kernel,
        grid=(n_heads, n_q),
        in_specs=[pl.BlockSpec(memory_space=pltpu.SMEM),
                  pl.BlockSpec((None, depth, tq), lambda h, i: (h, 0, i)),
                  pl.BlockSpec((None, depth, tq), lambda h, i: (h, 0, jnp.minimum(i + 1, n_q - 1))),
                  pl.BlockSpec((None, nb, tk, depth), lambda h, i: (h, 0, 0, 0)),
                  pl.BlockSpec((None, nb, v_rows, tk), lambda h, i: (h, 0, 0, 0))],
        out_specs=pl.BlockSpec((tq, HEAD_DIM), lambda h, i: (i, h)),
        out_shape=jax.ShapeDtypeStruct((s, n_heads * HEAD_DIM), BF16),
        scratch_shapes=[pltpu.VMEM((tk, tq), F32),
                        pltpu.VMEM((v_rows, tq), F32),
                        pltpu.VMEM((1, tq), F32)],
        compiler_params=_params(("arbitrary", "arbitrary"), vmem),
        name="flash",
    )(r, qt, qt, k, vt)


def _mix_kernel(x_ref, gc_ref, ga_ref, cb_ref, cc_ref, cv_ref, ccp_ref, cvp_ref, at_ref,
                cw_ref, bg_ref, wco_ref, wao_ref, wmo_ref, o_ref, ext_ref):
    i = pl.program_id(0)
    tm = x_ref.shape[0]
    halo = ccp_ref.shape[0]
    d = x_ref.shape[1]

    prev = ccp_ref[...].astype(F32) * cvp_ref[...].astype(F32)
    ext_ref[0:halo, :] = jnp.where(i == 0, 0.0, prev)
    ext_ref[halo:, :] = cc_ref[...].astype(F32) * cv_ref[...].astype(F32)
    y = None
    for tap in range(CONV_K):
        shifted = ext_ref[pl.ds(halo - (CONV_K - 1) + tap, tm), :]
        term = cw_ref[tap:tap + 1, :] * shifted
        y = term if y is None else y + term
    conv_y = (cb_ref[...].astype(F32) * y).astype(BF16)

    conv_branch = jnp.dot(conv_y, wco_ref[...], preferred_element_type=F32)
    attn_branch = jnp.dot(at_ref[...], wao_ref[...], preferred_element_type=F32)
    gate_c = jax.nn.sigmoid(gc_ref[...].astype(F32) + bg_ref[:, 0:d])
    gate_a = jax.nn.sigmoid(ga_ref[...].astype(F32) + bg_ref[:, d:])
    merged = (gate_c * conv_branch + gate_a * attn_branch).astype(BF16)
    o_ref[...] = x_ref[...] + jnp.dot(merged, wmo_ref[...], preferred_element_type=F32)


def _mix(x, z, attn, conv_w, b_gate, wco, wao, wmo):
    s, d = x.shape
    c = wco.shape[0]
    a = wao.shape[0]
    assert d == 2 * c and z.shape[1] == 2 * d + 3 * c
    tm = MIX_TM
    halo = V7X_BF16_SUBLANES
    assert s % tm == 0 and tm % halo == 0 and halo >= CONV_K - 1
    rows_per_tile = tm // halo
    prev_rows = lambda i: jnp.maximum(i * rows_per_tile - 1, 0)
    vmem = (4 * tm * d * 4 + 4 * tm * d * 2 + 6 * tm * c * 2 + 2 * tm * a * 2
            + (c + a + d) * d * 2 + (tm + halo) * c * 4
            + 4 * tm * d * 4 + 3 * tm * c * 4 + 4 * MIB)
    return pl.pallas_call(
        _mix_kernel,
        grid=(s // tm,),
        in_specs=[pl.BlockSpec((tm, d), lambda i: (i, 0)),
                  pl.BlockSpec((tm, d), lambda i: (i, 0)),
                  pl.BlockSpec((tm, d), lambda i: (i, 1)),
                  pl.BlockSpec((tm, c), lambda i: (i, 4)),
                  pl.BlockSpec((tm, c), lambda i: (i, 5)),
                  pl.BlockSpec((tm, c), lambda i: (i, 6)),
                  pl.BlockSpec((halo, c), lambda i: (prev_rows(i), 5)),
                  pl.BlockSpec((halo, c), lambda i: (prev_rows(i), 6)),
                  pl.BlockSpec((tm, a), lambda i: (i, 0)),
                  _resident((CONV_K, c), lambda i: (0, 0)),
                  _resident((1, 2 * d), lambda i: (0, 0)),
                  _resident((c, d), lambda i: (0, 0)),
                  _resident((a, d), lambda i: (0, 0)),
                  _resident((d, d), lambda i: (0, 0))],
        out_specs=pl.BlockSpec((tm, d), lambda i: (i, 0)),
        out_shape=jax.ShapeDtypeStruct((s, d), F32),
        scratch_shapes=[pltpu.VMEM((tm + halo, c), F32)],
        compiler_params=_params(("arbitrary",), vmem),
        name="mix",
    )(x, z, z, z, z, z, z, z, attn, conv_w, b_gate, wco, wao, wmo)


def _mlp_kernel(x_ref, g_ref, w1_ref, w2_ref, gf_ref, o_ref, h_ref, *, final_norm):
    j = pl.program_id(1)

    @pl.when(j == 0)
    def _():
        x = x_ref[...]
        h_ref[...] = _rmsnorm(x, g_ref[...]).astype(BF16)
        o_ref[...] = x

    a = jnp.dot(h_ref[...], w1_ref[...], preferred_element_type=F32)
    u = jnp.square(jnp.maximum(a, 0.0)).astype(BF16)
    o_ref[...] += jnp.dot(u, w2_ref[...], preferred_element_type=F32)

    if final_norm:
        @pl.when(j == pl.num_programs(1) - 1)
        def _():
            o_ref[...] = _rmsnorm(o_ref[...], gf_ref[...])


def _mlp(x, g, w1, w2, g_final, final_norm):
    s, d = x.shape
    ff = w1.shape[1]
    tm, tf = MLP_TM, MLP_TF
    assert s % tm == 0 and ff % tf == 0
    vmem = (4 * tm * d * 4 + tm * d * 2 + 4 * d * tf * 2
            + tm * tf * 6 + 2 * tm * d * 4 + 4 * MIB)
    return pl.pallas_call(
        functools.partial(_mlp_kernel, final_norm=final_norm),
        grid=(s // tm, ff // tf),
        in_specs=[pl.BlockSpec((tm, d), lambda i, j: (i, 0)),
                  pl.BlockSpec((1, d), lambda i, j: (0, 0)),
                  pl.BlockSpec((d, tf), lambda i, j: (0, j)),
                  pl.BlockSpec((tf, d), lambda i, j: (j, 0)),
                  pl.BlockSpec((1, d), lambda i, j: (0, 0))],
        out_specs=pl.BlockSpec((tm, d), lambda i, j: (i, 0)),
        out_shape=jax.ShapeDtypeStruct((s, d), F32),
        scratch_shapes=[pltpu.VMEM((tm, d), BF16)],
        compiler_params=_params(("arbitrary", "arbitrary"), vmem),
        name="mlp",
    )(x, g, w1, w2, g_final)


def kernel(x, g_mix, w_in, b_f, b_gate, conv_w, w_conv_out, w_attn_out, w_mix_out, g_mlp, w_ff1, w_ff2, g_final):
    b, s, d = x.shape
    depth = g_mix.shape[0]
    c = w_conv_out.shape[1]
    a = w_attn_out.shape[1]
    n_heads = b_f.shape[1]
    assert a == n_heads * HEAD_DIM and w_in.shape[2] == 3 * c + 3 * a + n_heads + 2 * d
    o_q, o_k, o_v, o_f, o_g = 3 * c, 3 * c + a, 3 * c + 2 * a, 3 * c + 3 * a, 3 * c + 3 * a + n_heads

    outs = []
    for bi in range(b):
        xs = x[bi]
        for l in range(depth):
            w = w_in[l]
            w_cg = jnp.concatenate([w[:, o_g:], w[:, :o_q]], axis=1).astype(BF16)
            wqt = w[:, o_q:o_k].T.astype(BF16)
            wk = w[:, o_k:o_v].astype(BF16)
            wvt = w[:, o_v:o_f].T.astype(BF16)
            wf = jnp.pad(w[:, o_f:o_g], ((0, 0), (0, V7X_LANES - n_heads))).astype(BF16)
            bf = jnp.pad(b_f[l], (0, V7X_LANES - n_heads)).reshape(1, V7X_LANES)
            g1 = g_mix[l].reshape(1, d)

            z = _inproj(xs, g1, w_cg)
            qt, k, vt, r = _attn_prep(xs, g1, wqt, wk, wvt, wf, bf)
            r_heads = r[:, 0, :n_heads].T
            attn = _flash(r_heads, qt, k, vt)
            xs = _mix(xs, z, attn, conv_w[l], b_gate[l].reshape(1, 2 * d),
                      w_conv_out[l].astype(BF16), w_attn_out[l].astype(BF16), w_mix_out[l].astype(BF16))
            xs = _mlp(xs, g_mlp[l].reshape(1, d), w_ff1[l].astype(BF16), w_ff2[l].astype(BF16),
                      g_final.reshape(1, d), final_norm=(l == depth - 1))
        outs.append(xs)
    return jnp.stack(outs, axis=0)
```

```python
import functools

import jax
import jax.numpy as jnp
from jax import lax
from jax.experimental import pallas as pl
from jax.experimental.pallas import tpu as pltpu

EPS = 1e-6
HEAD_DIM = 128
CONV_K = 3

V7X_LANES = 128
V7X_MXU_DEPTH = 256
V7X_F32_SUBLANES = 8
V7X_BF16_SUBLANES = 16
V7X_VMEM_BYTES = 64 * 1024 * 1024
MIB = 1024 * 1024

INPROJ_TM = 1024
INPROJ_TN = 1024
KV_BLOCK = 512
Q_BLOCK = 2 * KV_BLOCK
MIX_TM = 256
MLP_TM = 512
MLP_TF = 1024
SUM_ROWS = V7X_BF16_SUBLANES
V_ROWS = HEAD_DIM + SUM_ROWS

MASK_VALUE = -1e30
LOG2_E = 1.4426950408889634

BF16 = jnp.bfloat16
F32 = jnp.float32


def _rmsnorm(x, g):
    ms = jnp.mean(x * x, axis=-1, keepdims=True)
    return x * lax.rsqrt(ms + EPS) * g


def _split3_bf16(v):
    hi = v.astype(BF16).astype(F32)
    r1 = v - hi
    mid = r1.astype(BF16).astype(F32)
    lo = (r1 - mid).astype(BF16).astype(F32)
    return hi, mid, lo


def _params(semantics, vmem_bytes):
    assert vmem_bytes <= V7X_VMEM_BYTES - 4 * MIB, vmem_bytes
    return pltpu.CompilerParams(dimension_semantics=semantics, vmem_limit_bytes=int(vmem_bytes))


def _resident(block_shape, index_map):
    return pl.BlockSpec(block_shape, index_map, pipeline_mode=pl.Buffered(1))


def _inproj_kernel(x_ref, g_ref, w_ref, z_ref, h_ref):
    @pl.when(pl.program_id(1) == 0)
    def _():
        h_ref[...] = _rmsnorm(x_ref[...], g_ref[...]).astype(BF16)

    z_ref[...] = jnp.dot(h_ref[...], w_ref[...], preferred_element_type=F32).astype(z_ref.dtype)


def _inproj(x, g, w):
    s, d = x.shape
    n = w.shape[1]
    tm, tn = min(INPROJ_TM, s), INPROJ_TN
    assert s % tm == 0 and n % tn == 0
    vmem = (2 * tm * d * 4 + tm * d * 2 + 2 * d * tn * 2 + 2 * tm * tn * 2
            + tm * tn * 4 + tm * d * 4 + 4 * MIB)
    return pl.pallas_call(
        _inproj_kernel,
        grid=(s // tm, n // tn),
        in_specs=[pl.BlockSpec((tm, d), lambda i, j: (i, 0)),
                  pl.BlockSpec((1, d), lambda i, j: (0, 0)),
                  pl.BlockSpec((d, tn), lambda i, j: (0, j))],
        out_specs=pl.BlockSpec((tm, tn), lambda i, j: (i, j)),
        out_shape=jax.ShapeDtypeStruct((s, n), BF16),
        scratch_shapes=[pltpu.VMEM((tm, d), BF16)],
        compiler_params=_params(("arbitrary", "arbitrary"), vmem),
        name="inproj",
    )(x, g, w)


def _prep_kernel(x_ref, g_ref, wqt_ref, wk_ref, wvt_ref, wf_ref, bf_ref,
                 qt_ref, k_ref, vt_ref, r_ref, carry_ref, *, n_heads, scale):
    i = pl.program_id(0)
    tm = x_ref.shape[0]

    @pl.when(i == 0)
    def _():
        carry_ref[...] = jnp.zeros_like(carry_ref)

    h = _rmsnorm(x_ref[...], g_ref[...]).astype(BF16)
    nt = (((1,), (1,)), ((), ()))
    qt = lax.dot_general(wqt_ref[...], h, nt, preferred_element_type=F32) * scale
    vt = lax.dot_general(wvt_ref[...], h, nt, preferred_element_type=F32)
    k = jnp.dot(h, wk_ref[...], preferred_element_type=F32)
    f = jnp.dot(h, wf_ref[...], preferred_element_type=F32) + bf_ref[...]

    lane = lax.broadcasted_iota(jnp.int32, f.shape, 1)
    logf = jnp.where(lane < n_heads, jax.nn.log_sigmoid(f) * LOG2_E, 0.0)

    row = lax.broadcasted_iota(jnp.int32, (tm, tm), 0)
    col = lax.broadcasted_iota(jnp.int32, (tm, tm), 1)
    tril = (col <= row).astype(BF16)
    c_rel = None
    for part in _split3_bf16(logf):
        term = jnp.dot(tril, part.astype(BF16), preferred_element_type=F32)
        c_rel = term if c_rel is None else c_rel + term

    carry = carry_ref[...]
    r_ref[0] = carry
    carry_ref[...] = carry + c_rel[tm - 1:tm, :]

    nhi, nmid, nlo = _split3_bf16(-c_rel)
    bias_cols = (nhi + pltpu.roll(nmid, n_heads, 1) + pltpu.roll(nlo, 2 * n_heads, 1)).astype(BF16)

    sel_row = lax.broadcasted_iota(jnp.int32, (V7X_LANES, tm), 0)
    sum_rows = (lax.broadcasted_iota(jnp.int32, (SUM_ROWS, tm), 0) == 0).astype(BF16)
    for hh in range(n_heads):
        sl = slice(hh * HEAD_DIM, (hh + 1) * HEAD_DIM)
        onehot = ((sel_row == hh) | (sel_row == n_heads + hh) | (sel_row == 2 * n_heads + hh))
        qt_ref[hh, 0:HEAD_DIM, :] = qt[sl, :].astype(BF16)
        qt_ref[hh, HEAD_DIM:, :] = onehot.astype(BF16)
        k_ref[hh, 0, :, 0:HEAD_DIM] = k[:, sl].astype(BF16)
        k_ref[hh, 0, :, HEAD_DIM:] = bias_cols
        vt_ref[hh, 0, 0:HEAD_DIM, :] = vt[sl, :].astype(BF16)
        vt_ref[hh, 0, HEAD_DIM:, :] = sum_rows


def _attn_prep(x, g, wqt, wk, wvt, wf, bf):
    s, d = x.shape
    a = wk.shape[1]
    n_heads = a // HEAD_DIM
    assert 3 * n_heads <= V7X_LANES and 2 * HEAD_DIM == V7X_MXU_DEPTH
    tm = KV_BLOCK
    assert s % tm == 0
    nb = s // tm
    vmem = (2 * tm * d * 4 + 3 * d * a * 2 + d * V7X_LANES * 2
            + 2 * n_heads * tm * (2 * V7X_MXU_DEPTH + V_ROWS) * 2
            + tm * d * 6 + 3 * tm * a * 4 + tm * tm * 2 + 6 * MIB)
    kern = functools.partial(_prep_kernel, n_heads=n_heads, scale=LOG2_E * float(HEAD_DIM) ** -0.5)
    return pl.pallas_call(
        kern,
        grid=(nb,),
        in_specs=[pl.BlockSpec((tm, d), lambda i: (i, 0)),
                  _resident((1, d), lambda i: (0, 0)),
                  _resident((a, d), lambda i: (0, 0)),
                  _resident((d, a), lambda i: (0, 0)),
                  _resident((a, d), lambda i: (0, 0)),
                  _resident((d, V7X_LANES), lambda i: (0, 0)),
                  _resident((1, V7X_LANES), lambda i: (0, 0))],
        out_specs=[pl.BlockSpec((n_heads, V7X_MXU_DEPTH, tm), lambda i: (0, 0, i)),
                   pl.BlockSpec((n_heads, 1, tm, V7X_MXU_DEPTH), lambda i: (0, i, 0, 0)),
                   pl.BlockSpec((n_heads, 1, V_ROWS, tm), lambda i: (0, i, 0, 0)),
                   pl.BlockSpec((1, V7X_F32_SUBLANES, V7X_LANES), lambda i: (i, 0, 0))],
        out_shape=[jax.ShapeDtypeStruct((n_heads, V7X_MXU_DEPTH, s), BF16),
                   jax.ShapeDtypeStruct((n_heads, nb, tm, V7X_MXU_DEPTH), BF16),
                   jax.ShapeDtypeStruct((n_heads, nb, V_ROWS, tm), BF16),
                   jax.ShapeDtypeStruct((nb, V7X_F32_SUBLANES, V7X_LANES), F32)],
        scratch_shapes=[pltpu.VMEM((V7X_F32_SUBLANES, V7X_LANES), F32)],
        compiler_params=_params(("arbitrary",), vmem),
        name="attn_prep",
    )(x, g, wqt, wk, wvt, wf, bf)


def _flash_kernel(r_ref, qt_ref, k_ref, vt_ref, o_ref, s_ref, acc_ref):
    hd = pl.program_id(0)
    qi = pl.program_id(1)
    tk = k_ref.shape[1]
    tq = qt_ref.shape[1]

    sw = V7X_MXU_DEPTH
    strips = [slice(c * sw, (c + 1) * sw) for c in range(tq // sw)]

    def scores(blk, c):
        s = jnp.dot(k_ref[blk], qt_ref[:, strips[c]], preferred_element_type=F32)
        s_ref[:, strips[c]] = s
        return jnp.max(s, axis=0, keepdims=True)

    def attend(blk, c, m, bm, key0=None):
        s = s_ref[:, strips[c]]
        if key0 is not None:
            key = key0 + lax.broadcasted_iota(jnp.int32, (tk, sw), 0)
            qry = c * sw + lax.broadcasted_iota(jnp.int32, (tk, sw), 1)
            s = jnp.where(key <= qry, s, MASK_VALUE)
            bm = jnp.max(s, axis=0, keepdims=True)
        rb = r_ref[hd, blk]
        m_new = jnp.maximum(m, bm - rb)
        alpha = jnp.exp2(m - m_new)
        p = jnp.exp2(s - (m_new + rb)).astype(BF16)
        acc_ref[:, strips[c]] = alpha * acc_ref[:, strips[c]] + jnp.dot(
            vt_ref[blk], p, preferred_element_type=F32)
        return m_new

    def visible(key0, c):
        if key0 is None or key0 + tk - 1 <= c * sw:
            return "all"
        return "none" if key0 > (c + 1) * sw - 1 else "some"

    def block(t, carry, key0=None, next_key0=None, last=False):
        out = []
        for c in range(len(strips)):
            m, bm = carry[c]
            if visible(key0, c) != "none":
                m = attend(t, c, m, bm, key0 if visible(key0, c) == "some" else None)
            if not last and visible(next_key0, c) != "none":
                bm = scores(t + 1, c)
            out.append((m, bm))
        return tuple(out)

    acc_ref[...] = jnp.zeros_like(acc_ref)
    init = tuple((jnp.full((1, sw), MASK_VALUE, F32), scores(0, c)) for c in range(len(strips)))

    def blocks(first, count, carry):
        for j in range(count):
            carry = block(first + j, carry)
        return carry

    done = 8 * (qi // 4)
    carry = lax.fori_loop(0, qi // 4, lambda w, carry: blocks(8 * w, 8, carry), init)
    carry = lax.cond(qi % 4 >= 2, lambda carry: blocks(done, 4, carry), lambda carry: carry, carry)
    carry = lax.cond(qi % 2 == 1, lambda carry: blocks(2 * qi - 2, 2, carry), lambda carry: carry, carry)

    carry = block(2 * qi, carry, key0=0, next_key0=tk)
    block(2 * qi + 1, carry, key0=tk, last=True)
    inv_l = 1.0 / acc_ref[HEAD_DIM:HEAD_DIM + 1, :]
    o_ref[...] = (acc_ref[0:HEAD_DIM, :] * inv_l).T.astype(o_ref.dtype)


def _flash(r, qt, k, vt):
    n_heads, depth, s = qt.shape
    nb, tk = k.shape[1], k.shape[2]
    tq = Q_BLOCK
    assert tq == 2 * tk and s % tq == 0
    v_rows = vt.shape[2]
    vmem = (2 * nb * tk * (depth + v_rows) * 2 + 2 * depth * tq * 2 + 2 * tq * HEAD_DIM * 2
            + 2 * tk * tq * 6 + v_rows * tq * 4 + 3 * tk * tq * 4 + 4 * MIB)
    return pl.pallas_call(
        _flash_kernel,
        grid=(n_heads, s // tq),
        in_specs=[pl.BlockSpec(memory_space=pltpu.SMEM),
                  pl.BlockSpec((None, depth, tq), lambda h, i: (h, 0, i)),
                  pl.BlockSpec((None, nb, tk, depth), lambda h, i: (h, 0, 0, 0)),
                  pl.BlockSpec((None, nb, v_rows, tk), lambda h, i: (h, 0, 0, 0))],
        out_specs=pl.BlockSpec((tq, HEAD_DIM), lambda h, i: (i, h)),
        out_shape=jax.ShapeDtypeStruct((s, n_heads * HEAD_DIM), BF16),
        scratch_shapes=[pltpu.VMEM((tk, tq), F32),
                        pltpu.VMEM((v_rows, tq), F32)],
        compiler_params=_params(("arbitrary", "arbitrary"), vmem),
        name="flash",
    )(r, qt, k, vt)


def _mix_kernel(x_ref, gc_ref, ga_ref, cb_ref, cc_ref, cv_ref, ccp_ref, cvp_ref, at_ref,
                cw_ref, bg_ref, wco_ref, wao_ref, wmo_ref, o_ref, ext_ref):
    i = pl.program_id(0)
    tm = x_ref.shape[0]
    halo = ccp_ref.shape[0]
    d = x_ref.shape[1]

    prev = ccp_ref[...].astype(F32) * cvp_ref[...].astype(F32)
    ext_ref[0:halo, :] = jnp.where(i == 0, 0.0, prev)
    ext_ref[halo:, :] = cc_ref[...].astype(F32) * cv_ref[...].astype(F32)
    y = None
    for tap in range(CONV_K):
        shifted = ext_ref[pl.ds(halo - (CONV_K - 1) + tap, tm), :]
        term = cw_ref[tap:tap + 1, :] * shifted
        y = term if y is None else y + term
    conv_y = (cb_ref[...].astype(F32) * y).astype(BF16)

    conv_branch = jnp.dot(conv_y, wco_ref[...], preferred_element_type=F32)
    attn_branch = jnp.dot(at_ref[...], wao_ref[...], preferred_element_type=F32)
    gate_c = jax.nn.sigmoid(gc_ref[...].astype(F32) + bg_ref[:, 0:d])
    gate_a = jax.nn.sigmoid(ga_ref[...].astype(F32) + bg_ref[:, d:])
    merged = (gate_c * conv_branch + gate_a * attn_branch).astype(BF16)
    o_ref[...] = x_ref[...] + jnp.dot(merged, wmo_ref[...], preferred_element_type=F32)


def _mix(x, z, attn, conv_w, b_gate, wco, wao, wmo):
    s, d = x.shape
    c = wco.shape[0]
    a = wao.shape[0]
    assert d == 2 * c and z.shape[1] == 2 * d + 3 * c
    tm = MIX_TM
    halo = V7X_BF16_SUBLANES
    assert s % tm == 0 and tm % halo == 0 and halo >= CONV_K - 1
    rows_per_tile = tm // halo
    prev_rows = lambda i: jnp.maximum(i * rows_per_tile - 1, 0)
    vmem = (4 * tm * d * 4 + 4 * tm * d * 2 + 6 * tm * c * 2 + 2 * tm * a * 2
            + (c + a + d) * d * 2 + (tm + halo) * c * 4
            + 4 * tm * d * 4 + 3 * tm * c * 4 + 4 * MIB)
    return pl.pallas_call(
        _mix_kernel,
        grid=(s // tm,),
        in_specs=[pl.BlockSpec((tm, d), lambda i: (i, 0)),
                  pl.BlockSpec((tm, d), lambda i: (i, 0)),
                  pl.BlockSpec((tm, d), lambda i: (i, 1)),
                  pl.BlockSpec((tm, c), lambda i: (i, 4)),
                  pl.BlockSpec((tm, c), lambda i: (i, 5)),
                  pl.BlockSpec((tm, c), lambda i: (i, 6)),
                  pl.BlockSpec((halo, c), lambda i: (prev_rows(i), 5)),
                  pl.BlockSpec((halo, c), lambda i: (prev_rows(i), 6)),
                  pl.BlockSpec((tm, a), lambda i: (i, 0)),
                  _resident((CONV_K, c), lambda i: (0, 0)),
                  _resident((1, 2 * d), lambda i: (0, 0)),
                  _resident((c, d), lambda i: (0, 0)),
                  _resident((a, d), lambda i: (0, 0)),
                  _resident((d, d), lambda i: (0, 0))],
        out_specs=pl.BlockSpec((tm, d), lambda i: (i, 0)),
        out_shape=jax.ShapeDtypeStruct((s, d), F32),
        scratch_shapes=[pltpu.VMEM((tm + halo, c), F32)],
        compiler_params=_params(("arbitrary",), vmem),
        name="mix",
    )(x, z, z, z, z, z, z, z, attn, conv_w, b_gate, wco, wao, wmo)


def _mlp_kernel(x_ref, g_ref, w1_ref, w2_ref, gf_ref, o_ref, h_ref, *, final_norm):
    j = pl.program_id(1)

    @pl.when(j == 0)
    def _():
        x = x_ref[...]
        h_ref[...] = _rmsnorm(x, g_ref[...]).astype(BF16)
        o_ref[...] = x

    a = jnp.dot(h_ref[...], w1_ref[...], preferred_element_type=F32)
    u = jnp.square(jnp.maximum(a, 0.0)).astype(BF16)
    o_ref[...] += jnp.dot(u, w2_ref[...], preferred_element_type=F32)

    if final_norm:
        @pl.when(j == pl.num_programs(1) - 1)
        def _():
            o_ref[...] = _rmsnorm(o_ref[...], gf_ref[...])


def _mlp(x, g, w1, w2, g_final, final_norm):
    s, d = x.shape
    ff = w1.shape[1]
    tm, tf = MLP_TM, MLP_TF
    assert s % tm == 0 and ff % tf == 0
    vmem = (4 * tm * d * 4 + tm * d * 2 + 4 * d * tf * 2
            + tm * tf * 6 + 2 * tm * d * 4 + 4 * MIB)
    return pl.pallas_call(
        functools.partial(_mlp_kernel, final_norm=final_norm),
        grid=(s // tm, ff // tf),
        in_specs=[pl.BlockSpec((tm, d), lambda i, j: (i, 0)),
                  pl.BlockSpec((1, d), lambda i, j: (0, 0)),
                  pl.BlockSpec((d, tf), lambda i, j: (0, j)),
                  pl.BlockSpec((tf, d), lambda i, j: (j, 0)),
                  pl.BlockSpec((1, d), lambda i, j: (0, 0))],
        out_specs=pl.BlockSpec((tm, d), lambda i, j: (i, 0)),
        out_shape=jax.ShapeDtypeStruct((s, d), F32),
        scratch_shapes=[pltpu.VMEM((tm, d), BF16)],
        compiler_params=_params(("arbitrary", "arbitrary"), vmem),
        name="mlp",
    )(x, g, w1, w2, g_final)


def kernel(x, g_mix, w_in, b_f, b_gate, conv_w, w_conv_out, w_attn_out, w_mix_out, g_mlp, w_ff1, w_ff2, g_final):
    b, s, d = x.shape
    depth = g_mix.shape[0]
    c = w_conv_out.shape[1]
    a = w_attn_out.shape[1]
    n_heads = b_f.shape[1]
    assert a == n_heads * HEAD_DIM and w_in.shape[2] == 3 * c + 3 * a + n_heads + 2 * d
    o_q, o_k, o_v, o_f, o_g = 3 * c, 3 * c + a, 3 * c + 2 * a, 3 * c + 3 * a, 3 * c + 3 * a + n_heads

    outs = []
    for bi in range(b):
        xs = x[bi]
        for l in range(depth):
            w = w_in[l]
            w_cg = jnp.concatenate([w[:, o_g:], w[:, :o_q]], axis=1).astype(BF16)
            wqt = w[:, o_q:o_k].T.astype(BF16)
            wk = w[:, o_k:o_v].astype(BF16)
            wvt = w[:, o_v:o_f].T.astype(BF16)
            wf = jnp.pad(w[:, o_f:o_g], ((0, 0), (0, V7X_LANES - n_heads))).astype(BF16)
            bf = jnp.pad(b_f[l], (0, V7X_LANES - n_heads)).reshape(1, V7X_LANES)
            g1 = g_mix[l].reshape(1, d)

            z = _inproj(xs, g1, w_cg)
            qt, k, vt, r = _attn_prep(xs, g1, wqt, wk, wvt, wf, bf)
            r_heads = r[:, 0, :n_heads].T
            attn = _flash(r_heads, qt, k, vt)
            xs = _mix(xs, z, attn, conv_w[l], b_gate[l].reshape(1, 2 * d),
                      w_conv_out[l].astype(BF16), w_attn_out[l].astype(BF16), w_mix_out[l].astype(BF16))
            xs = _mlp(xs, g_mlp[l].reshape(1, d), w_ff1[l].astype(BF16), w_ff2[l].astype(BF16),
                      g_final.reshape(1, d), final_norm=(l == depth - 1))
        outs.append(xs)
    return jnp.stack(outs, axis=0)
```

```python
import functools

import jax
import jax.numpy as jnp
from jax import lax
from jax.experimental import pallas as pl
from jax.experimental.pallas import tpu as pltpu

EPS = 1e-6
HEAD_DIM = 128
CONV_K = 3

V7X_LANES = 128
V7X_MXU_DEPTH = 256
V7X_F32_SUBLANES = 8
V7X_BF16_SUBLANES = 16
V7X_VMEM_BYTES = 64 * 1024 * 1024
MIB = 1024 * 1024

INPROJ_TM = 1024
INPROJ_TN = 1792
KV_BLOCK = 512
Q_BLOCK = 2 * KV_BLOCK
MIX_TM = 256
MLP_TM = 512
MLP_TF = 1024
SUM_ROWS = V7X_BF16_SUBLANES
V_ROWS = HEAD_DIM + SUM_ROWS

MASK_VALUE = -1e30
LOG2_E = 1.4426950408889634

BF16 = jnp.bfloat16
F32 = jnp.float32


def _rmsnorm(x, g):
    ms = jnp.mean(x * x, axis=-1, keepdims=True)
    return x * lax.rsqrt(ms + EPS) * g


def _split3_bf16(v):
    hi = v.astype(BF16).astype(F32)
    r1 = v - hi
    mid = r1.astype(BF16).astype(F32)
    lo = (r1 - mid).astype(BF16).astype(F32)
    return hi, mid, lo


def _params(semantics, vmem_bytes):
    assert vmem_bytes <= V7X_VMEM_BYTES - 4 * MIB, vmem_bytes
    return pltpu.CompilerParams(dimension_semantics=semantics, vmem_limit_bytes=int(vmem_bytes))


def _resident(block_shape, index_map):
    return pl.BlockSpec(block_shape, index_map, pipeline_mode=pl.Buffered(1))


def _inproj_kernel(x_ref, g_ref, w_ref, z_ref, h_ref):
    @pl.when(pl.program_id(1) == 0)
    def _():
        h_ref[...] = _rmsnorm(x_ref[...], g_ref[...]).astype(BF16)

    z_ref[...] = jnp.dot(h_ref[...], w_ref[...], preferred_element_type=F32).astype(z_ref.dtype)


def _inproj(x, g, w):
    s, d = x.shape
    n = w.shape[1]
    tm, tn = min(INPROJ_TM, s), INPROJ_TN
    assert s % tm == 0 and n % tn == 0
    vmem = (2 * tm * d * 4 + tm * d * 2 + 2 * d * tn * 2 + 2 * tm * tn * 2
            + tm * tn * 4 + tm * d * 4 + 4 * MIB)
    return pl.pallas_call(
        _inproj_kernel,
        grid=(s // tm, n // tn),
        in_specs=[pl.BlockSpec((tm, d), lambda i, j: (i, 0)),
                  pl.BlockSpec((1, d), lambda i, j: (0, 0)),
                  pl.BlockSpec((d, tn), lambda i, j: (0, j))],
        out_specs=pl.BlockSpec((tm, tn), lambda i, j: (i, j)),
        out_shape=jax.ShapeDtypeStruct((s, n), BF16),
        scratch_shapes=[pltpu.VMEM((tm, d), BF16)],
        compiler_params=_params(("arbitrary", "arbitrary"), vmem),
        name="inproj",
    )(x, g, w)


def _prep_kernel(x_ref, g_ref, wqvt_ref, wkf_ref, bf_ref,
                 qt_ref, k_ref, vt_ref, r_ref, carry_ref, *, n_heads, scale):
    i = pl.program_id(0)
    tm = x_ref.shape[0]

    @pl.when(i == 0)
    def _():
        carry_ref[...] = jnp.zeros_like(carry_ref)

    h = _rmsnorm(x_ref[...], g_ref[...]).astype(BF16)
    a = n_heads * HEAD_DIM
    qvt = lax.dot_general(wqvt_ref[...], h, (((1,), (1,)), ((), ())), preferred_element_type=F32)
    kf = jnp.dot(h, wkf_ref[...], preferred_element_type=F32)
    qt = qvt[0:a, :] * scale
    vt = qvt[a:, :]
    k = kf[:, 0:a]
    f = kf[:, a:] + bf_ref[...]

    lane = lax.broadcasted_iota(jnp.int32, f.shape, 1)
    logf = jnp.where(lane < n_heads, jax.nn.log_sigmoid(f) * LOG2_E, 0.0)

    row = lax.broadcasted_iota(jnp.int32, (tm, tm), 0)
    col = lax.broadcasted_iota(jnp.int32, (tm, tm), 1)
    tril = (col <= row).astype(BF16)
    parts = jnp.concatenate(_split3_bf16(logf), axis=1).astype(BF16)
    sums = jnp.dot(tril, parts, preferred_element_type=F32)
    c_rel = sums[:, 0:V7X_LANES] + sums[:, V7X_LANES:2 * V7X_LANES] + sums[:, 2 * V7X_LANES:]

    carry = carry_ref[...]
    r_ref[0] = carry
    carry_ref[...] = carry + c_rel[tm - 1:tm, :]

    nhi, nmid, nlo = _split3_bf16(-c_rel)
    bias_cols = (nhi + pltpu.roll(nmid, n_heads, 1) + pltpu.roll(nlo, 2 * n_heads, 1)).astype(BF16)

    sel_row = lax.broadcasted_iota(jnp.int32, (V7X_LANES, tm), 0)
    sum_rows = (lax.broadcasted_iota(jnp.int32, (SUM_ROWS, tm), 0) == 0).astype(BF16)
    for hh in range(n_heads):
        sl = slice(hh * HEAD_DIM, (hh + 1) * HEAD_DIM)
        onehot = ((sel_row == hh) | (sel_row == n_heads + hh) | (sel_row == 2 * n_heads + hh))
        qt_ref[hh, 0:HEAD_DIM, :] = qt[sl, :].astype(BF16)
        qt_ref[hh, HEAD_DIM:, :] = onehot.astype(BF16)
        k_ref[hh, 0, :, 0:HEAD_DIM] = k[:, sl].astype(BF16)
        k_ref[hh, 0, :, HEAD_DIM:] = bias_cols
        vt_ref[hh, 0, 0:HEAD_DIM, :] = vt[sl, :].astype(BF16)
        vt_ref[hh, 0, HEAD_DIM:, :] = sum_rows


def _attn_prep(x, g, wqvt, wkf, bf):
    s, d = x.shape
    a = wqvt.shape[0] // 2
    assert wkf.shape == (d, a + V7X_LANES)
    n_heads = a // HEAD_DIM
    assert 3 * n_heads <= V7X_LANES and 2 * HEAD_DIM == V7X_MXU_DEPTH
    tm = KV_BLOCK
    assert s % tm == 0
    nb = s // tm
    vmem = (2 * tm * d * 4 + 3 * d * a * 2 + d * V7X_LANES * 2
            + 2 * n_heads * tm * (2 * V7X_MXU_DEPTH + V_ROWS) * 2
            + tm * d * 6 + 3 * tm * a * 4 + tm * tm * 2 + 6 * MIB)
    kern = functools.partial(_prep_kernel, n_heads=n_heads, scale=LOG2_E * float(HEAD_DIM) ** -0.5)
    return pl.pallas_call(
        kern,
        grid=(nb,),
        in_specs=[pl.BlockSpec((tm, d), lambda i: (i, 0)),
                  _resident((1, d), lambda i: (0, 0)),
                  _resident((2 * a, d), lambda i: (0, 0)),
                  _resident((d, a + V7X_LANES), lambda i: (0, 0)),
                  _resident((1, V7X_LANES), lambda i: (0, 0))],
        out_specs=[pl.BlockSpec((n_heads, V7X_MXU_DEPTH, tm), lambda i: (0, 0, i)),
                   pl.BlockSpec((n_heads, 1, tm, V7X_MXU_DEPTH), lambda i: (0, i, 0, 0)),
                   pl.BlockSpec((n_heads, 1, V_ROWS, tm), lambda i: (0, i, 0, 0)),
                   pl.BlockSpec((1, V7X_F32_SUBLANES, V7X_LANES), lambda i: (i, 0, 0))],
        out_shape=[jax.ShapeDtypeStruct((n_heads, V7X_MXU_DEPTH, s), BF16),
                   jax.ShapeDtypeStruct((n_heads, nb, tm, V7X_MXU_DEPTH), BF16),
                   jax.ShapeDtypeStruct((n_heads, nb, V_ROWS, tm), BF16),
                   jax.ShapeDtypeStruct((nb, V7X_F32_SUBLANES, V7X_LANES), F32)],
        scratch_shapes=[pltpu.VMEM((V7X_F32_SUBLANES, V7X_LANES), F32)],
        compiler_params=_params(("arbitrary",), vmem),
        name="attn_prep",
    )(x, g, wqvt, wkf, bf)


def _flash_kernel(r_ref, qt_ref, k_ref, vt_ref, o_ref, s_ref, acc_ref):
    hd = pl.program_id(0)
    qi = pl.program_id(1)
    tk = k_ref.shape[1]
    tq = qt_ref.shape[1]

    sw = V7X_MXU_DEPTH
    strips = [slice(c * sw, (c + 1) * sw) for c in range(tq // sw)]

    def scores(blk, c):
        s = jnp.dot(k_ref[blk], qt_ref[:, strips[c]], preferred_element_type=F32)
        s_ref[:, strips[c]] = s
        return jnp.max(s, axis=0, keepdims=True)

    def attend(blk, c, m, bm, key0=None):
        s = s_ref[:, strips[c]]
        if key0 is not None:
            key = key0 + lax.broadcasted_iota(jnp.int32, (tk, sw), 0)
            qry = c * sw + lax.broadcasted_iota(jnp.int32, (tk, sw), 1)
            s = jnp.where(key <= qry, s, MASK_VALUE)
            bm = jnp.max(s, axis=0, keepdims=True)
        rb = r_ref[hd, blk]
        m_new = jnp.maximum(m, bm - rb)
        alpha = jnp.exp2(m - m_new)
        p = jnp.exp2(s - (m_new + rb)).astype(BF16)
        acc_ref[:, strips[c]] = alpha * acc_ref[:, strips[c]] + jnp.dot(
            vt_ref[blk], p, preferred_element_type=F32)
        return m_new

    def visible(key0, c):
        if key0 is None or key0 + tk - 1 <= c * sw:
            return "all"
        return "none" if key0 > (c + 1) * sw - 1 else "some"

    def block(t, carry, key0=None, next_key0=None, last=False):
        out = []
        for c in range(len(strips)):
            m, bm = carry[c]
            if visible(key0, c) != "none":
                m = attend(t, c, m, bm, key0 if visible(key0, c) == "some" else None)
            if not last and visible(next_key0, c) != "none":
                bm = scores(t + 1, c)
            out.append((m, bm))
        return tuple(out)

    acc_ref[...] = jnp.zeros_like(acc_ref)
    init = tuple((jnp.full((1, sw), MASK_VALUE, F32), scores(0, c)) for c in range(len(strips)))

    def blocks(first, count, carry):
        for j in range(count):
            carry = block(first + j, carry)
        return carry

    done = 8 * (qi // 4)
    carry = lax.fori_loop(0, qi // 4, lambda w, carry: blocks(8 * w, 8, carry), init)
    carry = lax.cond(qi % 4 >= 2, lambda carry: blocks(done, 4, carry), lambda carry: carry, carry)
    carry = lax.cond(qi % 2 == 1, lambda carry: blocks(2 * qi - 2, 2, carry), lambda carry: carry, carry)

    carry = block(2 * qi, carry, key0=0, next_key0=tk)
    block(2 * qi + 1, carry, key0=tk, last=True)
    inv_l = 1.0 / acc_ref[HEAD_DIM:HEAD_DIM + 1, :]
    o_ref[...] = (acc_ref[0:HEAD_DIM, :] * inv_l).T.astype(o_ref.dtype)


def _flash(r, qt, k, vt):
    n_heads, depth, s = qt.shape
    nb, tk = k.shape[1], k.shape[2]
    tq = Q_BLOCK
    assert tq == 2 * tk and s % tq == 0
    v_rows = vt.shape[2]
    vmem = (2 * nb * tk * (depth + v_rows) * 2 + 2 * depth * tq * 2 + 2 * tq * HEAD_DIM * 2
            + 2 * tk * tq * 6 + v_rows * tq * 4 + 3 * tk * tq * 4 + 4 * MIB)
    return pl.pallas_call(
        _flash_kernel,
        grid=(n_heads, s // tq),
        in_specs=[pl.BlockSpec(memory_space=pltpu.SMEM),
                  pl.BlockSpec((None, depth, tq), lambda h, i: (h, 0, i)),
                  pl.BlockSpec((None, nb, tk, depth), lambda h, i: (h, 0, 0, 0)),
                  pl.BlockSpec((None, nb, v_rows, tk), lambda h, i: (h, 0, 0, 0))],
        out_specs=pl.BlockSpec((tq, HEAD_DIM), lambda h, i: (i, h)),
        out_shape=jax.ShapeDtypeStruct((s, n_heads * HEAD_DIM), BF16),
        scratch_shapes=[pltpu.VMEM((tk, tq), F32),
                        pltpu.VMEM((v_rows, tq), F32)],
        compiler_params=_params(("arbitrary", "arbitrary"), vmem),
        name="flash",
    )(r, qt, k, vt)


def _mix_kernel(x_ref, gc_ref, ga_ref, cb_ref, cc_ref, cv_ref, ccp_ref, cvp_ref, at_ref,
                cw_ref, bg_ref, wco_ref, wao_ref, wmo_ref, o_ref, ext_ref):
    i = pl.program_id(0)
    tm = x_ref.shape[0]
    halo = ccp_ref.shape[0]
    d = x_ref.shape[1]

    prev = ccp_ref[...].astype(F32) * cvp_ref[...].astype(F32)
    ext_ref[0:halo, :] = jnp.where(i == 0, 0.0, prev)
    ext_ref[halo:, :] = cc_ref[...].astype(F32) * cv_ref[...].astype(F32)
    y = None
    for tap in range(CONV_K):
        shifted = ext_ref[pl.ds(halo - (CONV_K - 1) + tap, tm), :]
        term = cw_ref[tap:tap + 1, :] * shifted
        y = term if y is None else y + term
    conv_y = (cb_ref[...].astype(F32) * y).astype(BF16)

    conv_branch = jnp.dot(conv_y, wco_ref[...], preferred_element_type=F32)
    attn_branch = jnp.dot(at_ref[...], wao_ref[...], preferred_element_type=F32)
    gate_c = jax.nn.sigmoid(gc_ref[...].astype(F32) + bg_ref[:, 0:d])
    gate_a = jax.nn.sigmoid(ga_ref[...].astype(F32) + bg_ref[:, d:])
    merged = (gate_c * conv_branch + gate_a * attn_branch).astype(BF16)
    o_ref[...] = x_ref[...] + jnp.dot(merged, wmo_ref[...], preferred_element_type=F32)


def _mix(x, z, attn, conv_w, b_gate, wco, wao, wmo):
    s, d = x.shape
    c = wco.shape[0]
    a = wao.shape[0]
    assert d == 2 * c and z.shape[1] == 2 * d + 3 * c
    tm = MIX_TM
    halo = V7X_BF16_SUBLANES
    assert s % tm == 0 and tm % halo == 0 and halo >= CONV_K - 1
    rows_per_tile = tm // halo
    prev_rows = lambda i: jnp.maximum(i * rows_per_tile - 1, 0)
    vmem = (4 * tm * d * 4 + 4 * tm * d * 2 + 6 * tm * c * 2 + 2 * tm * a * 2
            + (c + a + d) * d * 2 + (tm + halo) * c * 4
            + 4 * tm * d * 4 + 3 * tm * c * 4 + 4 * MIB)
    return pl.pallas_call(
        _mix_kernel,
        grid=(s // tm,),
        in_specs=[pl.BlockSpec((tm, d), lambda i: (i, 0)),
                  pl.BlockSpec((tm, d), lambda i: (i, 0)),
                  pl.BlockSpec((tm, d), lambda i: (i, 1)),
                  pl.BlockSpec((tm, c), lambda i: (i, 4)),
                  pl.BlockSpec((tm, c), lambda i: (i, 5)),
                  pl.BlockSpec((tm, c), lambda i: (i, 6)),
                  pl.BlockSpec((halo, c), lambda i: (prev_rows(i), 5)),
                  pl.BlockSpec((halo, c), lambda i: (prev_rows(i), 6)),
                  pl.BlockSpec((tm, a), lambda i: (i, 0)),
                  _resident((CONV_K, c), lambda i: (0, 0)),
                  _resident((1, 2 * d), lambda i: (0, 0)),
                  _resident((c, d), lambda i: (0, 0)),
                  _resident((a, d), lambda i: (0, 0)),
                  _resident((d, d), lambda i: (0, 0))],
        out_specs=pl.BlockSpec((tm, d), lambda i: (i, 0)),
        out_shape=jax.ShapeDtypeStruct((s, d), F32),
        scratch_shapes=[pltpu.VMEM((tm + halo, c), F32)],
        compiler_params=_params(("arbitrary",), vmem),
        name="mix",
    )(x, z, z, z, z, z, z, z, attn, conv_w, b_gate, wco, wao, wmo)


def _mlp_kernel(x_ref, g_ref, w1_ref, w2_ref, gf_ref, o_ref, h_ref, *, final_norm):
    j = pl.program_id(1)

    @pl.when(j == 0)
    def _():
        x = x_ref[...]
        h_ref[...] = _rmsnorm(x, g_ref[...]).astype(BF16)
        o_ref[...] = x

    a = jnp.dot(h_ref[...], w1_ref[...], preferred_element_type=F32)
    u = jnp.square(jnp.maximum(a, 0.0)).astype(BF16)
    o_ref[...] += jnp.dot(u, w2_ref[...], preferred_element_type=F32)

    if final_norm:
        @pl.when(j == pl.num_programs(1) - 1)
        def _():
            o_ref[...] = _rmsnorm(o_ref[...], gf_ref[...])


def _mlp(x, g, w1, w2, g_final, final_norm):
    s, d = x.shape
    ff = w1.shape[1]
    tm, tf = MLP_TM, MLP_TF
    assert s % tm == 0 and ff % tf == 0
    vmem = (4 * tm * d * 4 + tm * d * 2 + 4 * d * tf * 2
            + tm * tf * 6 + 2 * tm * d * 4 + 4 * MIB)
    return pl.pallas_call(
        functools.partial(_mlp_kernel, final_norm=final_norm),
        grid=(s // tm, ff // tf),
        in_specs=[pl.BlockSpec((tm, d), lambda i, j: (i, 0)),
                  pl.BlockSpec((1, d), lambda i, j: (0, 0)),
                  pl.BlockSpec((d, tf), lambda i, j: (0, j)),
                  pl.BlockSpec((tf, d), lambda i, j: (j, 0)),
                  pl.BlockSpec((1, d), lambda i, j: (0, 0))],
        out_specs=pl.BlockSpec((tm, d), lambda i, j: (i, 0)),
        out_shape=jax.ShapeDtypeStruct((s, d), F32),
        scratch_shapes=[pltpu.VMEM((tm, d), BF16)],
        compiler_params=_params(("arbitrary", "arbitrary"), vmem),
        name="mlp",
    )(x, g, w1, w2, g_final)


def kernel(x, g_mix, w_in, b_f, b_gate, conv_w, w_conv_out, w_attn_out, w_mix_out, g_mlp, w_ff1, w_ff2, g_final):
    b, s, d = x.shape
    depth = g_mix.shape[0]
    c = w_conv_out.shape[1]
    a = w_attn_out.shape[1]
    n_heads = b_f.shape[1]
    assert a == n_heads * HEAD_DIM and w_in.shape[2] == 3 * c + 3 * a + n_heads + 2 * d
    o_q, o_k, o_v, o_f, o_g = 3 * c, 3 * c + a, 3 * c + 2 * a, 3 * c + 3 * a, 3 * c + 3 * a + n_heads

    outs = []
    for bi in range(b):
        xs = x[bi]
        for l in range(depth):
            w = w_in[l]
            w_cg = jnp.concatenate([w[:, o_g:], w[:, :o_q]], axis=1).astype(BF16)
            wqvt = jnp.concatenate([w[:, o_q:o_k], w[:, o_v:o_f]], axis=1).T.astype(BF16)
            wf = jnp.pad(w[:, o_f:o_g], ((0, 0), (0, V7X_LANES - n_heads)))
            wkf = jnp.concatenate([w[:, o_k:o_v], wf], axis=1).astype(BF16)
            bf = jnp.pad(b_f[l], (0, V7X_LANES - n_heads)).reshape(1, V7X_LANES)
            g1 = g_mix[l].reshape(1, d)

            z = _inproj(xs, g1, w_cg)
            qt, k, vt, r = _attn_prep(xs, g1, wqvt, wkf, bf)
            r_heads = r[:, 0, :n_heads].T
            attn = _flash(r_heads, qt, k, vt)
            xs = _mix(xs, z, attn, conv_w[l], b_gate[l].reshape(1, 2 * d),
                      w_conv_out[l].astype(BF16), w_attn_out[l].astype(BF16), w_mix_out[l].astype(BF16))
            xs = _mlp(xs, g_mlp[l].reshape(1, d), w_ff1[l].astype(BF16), w_ff2[l].astype(BF16),
                      g_final.reshape(1, d), final_norm=(l == depth - 1))
        outs.append(xs)
    return jnp.stack(outs, axis=0)
```

```python
import functools

import jax
import jax.numpy as jnp
from jax import lax
from jax.experimental import pallas as pl
from jax.experimental.pallas import tpu as pltpu

EPS = 1e-6
HEAD_DIM = 128
CONV_K = 3

V7X_LANES = 128
V7X_MXU_DEPTH = 256
V7X_F32_SUBLANES = 8
V7X_BF16_SUBLANES = 16
V7X_VMEM_BYTES = 64 * 1024 * 1024
MIB = 1024 * 1024

INPROJ_TM = 1024
INPROJ_TN = 1792
KV_BLOCK = 512
Q_BLOCK = 2 * KV_BLOCK
MIX_TM = 256
MLP_TM = 512
MLP_TF = 1024
SUM_ROWS = V7X_BF16_SUBLANES
V_ROWS = HEAD_DIM + SUM_ROWS

MASK_VALUE = -1e30
LOG2_E = 1.4426950408889634

BF16 = jnp.bfloat16
F32 = jnp.float32


def _rmsnorm(x, g):
    ms = jnp.mean(x * x, axis=-1, keepdims=True)
    return x * lax.rsqrt(ms + EPS) * g


def _split3_bf16(v):
    hi = v.astype(BF16).astype(F32)
    r1 = v - hi
    mid = r1.astype(BF16).astype(F32)
    lo = (r1 - mid).astype(BF16).astype(F32)
    return hi, mid, lo


def _params(semantics, vmem_bytes):
    assert vmem_bytes <= V7X_VMEM_BYTES - 4 * MIB, vmem_bytes
    return pltpu.CompilerParams(dimension_semantics=semantics, vmem_limit_bytes=int(vmem_bytes))


def _resident(block_shape, index_map):
    return pl.BlockSpec(block_shape, index_map, pipeline_mode=pl.Buffered(1))


def _inproj_kernel(x_ref, g_ref, w_ref, z_ref, h_ref):
    @pl.when(pl.program_id(1) == 0)
    def _():
        h_ref[...] = _rmsnorm(x_ref[...], g_ref[...]).astype(BF16)

    z_ref[...] = jnp.dot(h_ref[...], w_ref[...], preferred_element_type=F32).astype(z_ref.dtype)


def _inproj(x, g, w):
    s, d = x.shape
    n = w.shape[1]
    tm, tn = min(INPROJ_TM, s), INPROJ_TN
    assert s % tm == 0 and n % tn == 0
    vmem = (2 * tm * d * 4 + tm * d * 2 + 2 * d * tn * 2 + 2 * tm * tn * 2
            + tm * tn * 4 + tm * d * 4 + 4 * MIB)
    return pl.pallas_call(
        _inproj_kernel,
        grid=(s // tm, n // tn),
        in_specs=[pl.BlockSpec((tm, d), lambda i, j: (i, 0)),
                  pl.BlockSpec((1, d), lambda i, j: (0, 0)),
                  pl.BlockSpec((d, tn), lambda i, j: (0, j))],
        out_specs=pl.BlockSpec((tm, tn), lambda i, j: (i, j)),
        out_shape=jax.ShapeDtypeStruct((s, n), BF16),
        scratch_shapes=[pltpu.VMEM((tm, d), BF16)],
        compiler_params=_params(("arbitrary", "arbitrary"), vmem),
        name="inproj",
    )(x, g, w)


def _prep_kernel(x_ref, g_ref, wqvt_ref, wkf_ref, bf_ref,
                 qt_ref, k_ref, vt_ref, r_ref, carry_ref, *, n_heads, scale):
    i = pl.program_id(0)
    tm = x_ref.shape[0]

    @pl.when(i == 0)
    def _():
        carry_ref[...] = jnp.zeros_like(carry_ref)

    h = _rmsnorm(x_ref[...], g_ref[...]).astype(BF16)
    a = n_heads * HEAD_DIM
    qvt = lax.dot_general(wqvt_ref[...], h, (((1,), (1,)), ((), ())), preferred_element_type=F32)
    kf = jnp.dot(h, wkf_ref[...], preferred_element_type=F32)
    qt = qvt[0:a, :] * scale
    vt = qvt[a:, :]
    k = kf[:, 0:a]
    f = kf[:, a:] + bf_ref[...]

    lane = lax.broadcasted_iota(jnp.int32, f.shape, 1)
    logf = jnp.where(lane < n_heads, jax.nn.log_sigmoid(f) * LOG2_E, 0.0)

    row = lax.broadcasted_iota(jnp.int32, (tm, tm), 0)
    col = lax.broadcasted_iota(jnp.int32, (tm, tm), 1)
    tril = (col <= row).astype(BF16)
    parts = jnp.concatenate(_split3_bf16(logf), axis=1).astype(BF16)
    sums = jnp.dot(tril, parts, preferred_element_type=F32)
    c_rel = sums[:, 0:V7X_LANES] + sums[:, V7X_LANES:2 * V7X_LANES] + sums[:, 2 * V7X_LANES:]

    carry = carry_ref[...]
    r_ref[0] = carry
    carry_ref[...] = carry + c_rel[tm - 1:tm, :]

    nhi, nmid, nlo = _split3_bf16(-c_rel)
    bias_cols = (nhi + pltpu.roll(nmid, n_heads, 1) + pltpu.roll(nlo, 2 * n_heads, 1)).astype(BF16)

    sel_row = lax.broadcasted_iota(jnp.int32, (V7X_LANES, tm), 0)
    sum_rows = (lax.broadcasted_iota(jnp.int32, (SUM_ROWS, tm), 0) == 0).astype(BF16)
    for hh in range(n_heads):
        sl = slice(hh * HEAD_DIM, (hh + 1) * HEAD_DIM)
        onehot = ((sel_row == hh) | (sel_row == n_heads + hh) | (sel_row == 2 * n_heads + hh))
        qt_ref[hh, 0:HEAD_DIM, :] = qt[sl, :].astype(BF16)
        qt_ref[hh, HEAD_DIM:, :] = onehot.astype(BF16)
        k_ref[hh, 0, :, 0:HEAD_DIM] = k[:, sl].astype(BF16)
        k_ref[hh, 0, :, HEAD_DIM:] = bias_cols
        vt_ref[hh, 0, 0:HEAD_DIM, :] = vt[sl, :].astype(BF16)
        vt_ref[hh, 0, HEAD_DIM:, :] = sum_rows


def _attn_prep(x, g, wqvt, wkf, bf):
    s, d = x.shape
    a = wqvt.shape[0] // 2
    assert wkf.shape == (d, a + V7X_LANES)
    n_heads = a // HEAD_DIM
    assert 3 * n_heads <= V7X_LANES and 2 * HEAD_DIM == V7X_MXU_DEPTH
    tm = KV_BLOCK
    assert s % tm == 0
    nb = s // tm
    vmem = (2 * tm * d * 4 + 3 * d * a * 2 + d * V7X_LANES * 2
            + 2 * n_heads * tm * (2 * V7X_MXU_DEPTH + V_ROWS) * 2
            + tm * d * 6 + 3 * tm * a * 4 + tm * tm * 2 + 6 * MIB)
    kern = functools.partial(_prep_kernel, n_heads=n_heads, scale=LOG2_E * float(HEAD_DIM) ** -0.5)
    return pl.pallas_call(
        kern,
        grid=(nb,),
        in_specs=[pl.BlockSpec((tm, d), lambda i: (i, 0)),
                  _resident((1, d), lambda i: (0, 0)),
                  _resident((2 * a, d), lambda i: (0, 0)),
                  _resident((d, a + V7X_LANES), lambda i: (0, 0)),
                  _resident((1, V7X_LANES), lambda i: (0, 0))],
        out_specs=[pl.BlockSpec((n_heads, V7X_MXU_DEPTH, tm), lambda i: (0, 0, i)),
                   pl.BlockSpec((n_heads, 1, tm, V7X_MXU_DEPTH), lambda i: (0, i, 0, 0)),
                   pl.BlockSpec((n_heads, 1, V_ROWS, tm), lambda i: (0, i, 0, 0)),
                   pl.BlockSpec((1, V7X_F32_SUBLANES, V7X_LANES), lambda i: (i, 0, 0))],
        out_shape=[jax.ShapeDtypeStruct((n_heads, V7X_MXU_DEPTH, s), BF16),
                   jax.ShapeDtypeStruct((n_heads, nb, tm, V7X_MXU_DEPTH), BF16),
                   jax.ShapeDtypeStruct((n_heads, nb, V_ROWS, tm), BF16),
                   jax.ShapeDtypeStruct((nb, V7X_F32_SUBLANES, V7X_LANES), F32)],
        scratch_shapes=[pltpu.VMEM((V7X_F32_SUBLANES, V7X_LANES), F32)],
        compiler_params=_params(("arbitrary",), vmem),
        name="attn_prep",
    )(x, g, wqvt, wkf, bf)


def _flash_kernel(r_ref, qt_ref, k_ref, vt_ref, o_ref, s_ref, acc_ref):
    hd = pl.program_id(0)
    qi = pl.program_id(1)
    tk = k_ref.shape[1]
    tq = qt_ref.shape[1]

    sw = V7X_MXU_DEPTH
    strips = [slice(c * sw, (c + 1) * sw) for c in range(tq // sw)]

    def scores(blk, c):
        s = jnp.dot(k_ref[blk], qt_ref[:, strips[c]], preferred_element_type=F32)
        s_ref[:, strips[c]] = s
        return jnp.max(s, axis=0, keepdims=True)

    def attend(blk, c, m, bm, key0=None):
        s = s_ref[:, strips[c]]
        if key0 is not None:
            key = key0 + lax.broadcasted_iota(jnp.int32, (tk, sw), 0)
            qry = c * sw + lax.broadcasted_iota(jnp.int32, (tk, sw), 1)
            s = jnp.where(key <= qry, s, MASK_VALUE)
            bm = jnp.max(s, axis=0, keepdims=True)
        rb = r_ref[hd, blk]
        m_new = jnp.maximum(m, bm - rb)
        alpha = jnp.exp2(m - m_new)
        p = jnp.exp2(s - (m_new + rb)).astype(BF16)
        acc_ref[:, strips[c]] = alpha * acc_ref[:, strips[c]] + jnp.dot(
            vt_ref[blk], p, preferred_element_type=F32)
        return m_new

    def visible(key0, c):
        if key0 is None or key0 + tk - 1 <= c * sw:
            return "all"
        return "none" if key0 > (c + 1) * sw - 1 else "some"

    def block(t, carry, key0=None, next_key0=None, last=False):
        out = []
        for c in range(len(strips)):
            m, bm = carry[c]
            if visible(key0, c) != "none":
                m = attend(t, c, m, bm, key0 if visible(key0, c) == "some" else None)
            if not last and visible(next_key0, c) != "none":
                bm = scores(t + 1, c)
            out.append((m, bm))
        return tuple(out)

    acc_ref[...] = jnp.zeros_like(acc_ref)
    init = tuple((jnp.full((1, sw), MASK_VALUE, F32), scores(0, c)) for c in range(len(strips)))

    def blocks(first, count, carry):
        for j in range(count):
            carry = block(first + j, carry)
        return carry

    done = 8 * (qi // 4)
    carry = lax.fori_loop(0, qi // 4, lambda w, carry: blocks(8 * w, 8, carry), init)
    carry = lax.cond(qi % 4 >= 2, lambda carry: blocks(done, 4, carry), lambda carry: carry, carry)
    carry = lax.cond(qi % 2 == 1, lambda carry: blocks(2 * qi - 2, 2, carry), lambda carry: carry, carry)

    carry = block(2 * qi, carry, key0=0, next_key0=tk)
    block(2 * qi + 1, carry, key0=tk, last=True)
    inv_l = 1.0 / acc_ref[HEAD_DIM:HEAD_DIM + 1, :]
    o_ref[...] = (acc_ref[0:HEAD_DIM, :] * inv_l).T.astype(o_ref.dtype)


def _flash(r, qt, k, vt):
    n_heads, depth, s = qt.shape
    nb, tk = k.shape[1], k.shape[2]
    tq = Q_BLOCK
    assert tq == 2 * tk and s % tq == 0
    v_rows = vt.shape[2]
    vmem = (2 * nb * tk * (depth + v_rows) * 2 + 2 * depth * tq * 2 + 2 * tq * HEAD_DIM * 2
            + 2 * tk * tq * 6 + v_rows * tq * 4 + 3 * tk * tq * 4 + 4 * MIB)
    return pl.pallas_call(
        _flash_kernel,
        grid=(n_heads, s // tq),
        in_specs=[pl.BlockSpec(memory_space=pltpu.SMEM),
                  pl.BlockSpec((None, depth, tq), lambda h, i: (h, 0, i)),
                  pl.BlockSpec((None, nb, tk, depth), lambda h, i: (h, 0, 0, 0)),
                  pl.BlockSpec((None, nb, v_rows, tk), lambda h, i: (h, 0, 0, 0))],
        out_specs=pl.BlockSpec((tq, HEAD_DIM), lambda h, i: (i, h)),
        out_shape=jax.ShapeDtypeStruct((s, n_heads * HEAD_DIM), BF16),
        scratch_shapes=[pltpu.VMEM((tk, tq), F32),
                        pltpu.VMEM((v_rows, tq), F32)],
        compiler_params=_params(("arbitrary", "arbitrary"), vmem),
        name="flash",
    )(r, qt, k, vt)


def _mix_kernel(x_ref, gc_ref, ga_ref, cb_ref, cc_ref, cv_ref, ccp_ref, cvp_ref, at_ref,
                cw_ref, bg_ref, wco_ref, wao_ref, wmo_ref, o_ref, ext_ref):
    i = pl.program_id(0)
    tm = x_ref.shape[0]
    halo = ccp_ref.shape[0]
    d = x_ref.shape[1]

    prev = ccp_ref[...].astype(F32) * cvp_ref[...].astype(F32)
    ext_ref[0:halo, :] = jnp.where(i == 0, 0.0, prev)
    ext_ref[halo:, :] = cc_ref[...].astype(F32) * cv_ref[...].astype(F32)
    y = None
    for tap in range(CONV_K):
        shifted = ext_ref[pl.ds(halo - (CONV_K - 1) + tap, tm), :]
        term = cw_ref[tap:tap + 1, :] * shifted
        y = term if y is None else y + term
    conv_y = (cb_ref[...].astype(F32) * y).astype(BF16)

    conv_branch = jnp.dot(conv_y, wco_ref[...], preferred_element_type=F32)
    attn_branch = jnp.dot(at_ref[...], wao_ref[...], preferred_element_type=F32)
    gate_c = jax.nn.sigmoid(gc_ref[...].astype(F32) + bg_ref[:, 0:d])
    gate_a = jax.nn.sigmoid(ga_ref[...].astype(F32) + bg_ref[:, d:])
    merged = (gate_c * conv_branch + gate_a * attn_branch).astype(BF16)
    o_ref[...] = x_ref[...] + jnp.dot(merged, wmo_ref[...], preferred_element_type=F32)


def _mix(x, z, attn, conv_w, b_gate, wco, wao, wmo):
    s, d = x.shape
    c = wco.shape[0]
    a = wao.shape[0]
    assert d == 2 * c and z.shape[1] == 2 * d + 3 * c
    tm = MIX_TM
    halo = V7X_BF16_SUBLANES
    assert s % tm == 0 and tm % halo == 0 and halo >= CONV_K - 1
    rows_per_tile = tm // halo
    prev_rows = lambda i: jnp.maximum(i * rows_per_tile - 1, 0)
    vmem = (4 * tm * d * 4 + 4 * tm * d * 2 + 6 * tm * c * 2 + 2 * tm * a * 2
            + (c + a + d) * d * 2 + (tm + halo) * c * 4
            + 4 * tm * d * 4 + 3 * tm * c * 4 + 4 * MIB)
    return pl.pallas_call(
        _mix_kernel,
        grid=(s // tm,),
        in_specs=[pl.BlockSpec((tm, d), lambda i: (i, 0)),
                  pl.BlockSpec((tm, d), lambda i: (i, 0)),
                  pl.BlockSpec((tm, d), lambda i: (i, 1)),
                  pl.BlockSpec((tm, c), lambda i: (i, 4)),
                  pl.BlockSpec((tm, c), lambda i: (i, 5)),
                  pl.BlockSpec((tm, c), lambda i: (i, 6)),
                  pl.BlockSpec((halo, c), lambda i: (prev_rows(i), 5)),
                  pl.BlockSpec((halo, c), lambda i: (prev_rows(i), 6)),
                  pl.BlockSpec((tm, a), lambda i: (i, 0)),
                  _resident((CONV_K, c), lambda i: (0, 0)),
                  _resident((1, 2 * d), lambda i: (0, 0)),
                  _resident((c, d), lambda i: (0, 0)),
                  _resident((a, d), lambda i: (0, 0)),
                  _resident((d, d), lambda i: (0, 0))],
        out_specs=pl.BlockSpec((tm, d), lambda i: (i, 0)),
        out_shape=jax.ShapeDtypeStruct((s, d), F32),
        scratch_shapes=[pltpu.VMEM((tm + halo, c), F32)],
        compiler_params=_params(("arbitrary",), vmem),
        name="mix",
    )(x, z, z, z, z, z, z, z, attn, conv_w, b_gate, wco, wao, wmo)


def _mlp_kernel(x_ref, g_ref, w1_ref, w2_ref, gf_ref, o_ref, h_ref, *, final_norm):
    j = pl.program_id(1)

    @pl.when(j == 0)
    def _():
        x = x_ref[...]
        h_ref[...] = _rmsnorm(x, g_ref[...]).astype(BF16)
        o_ref[...] = x

    a = jnp.dot(h_ref[...], w1_ref[...], preferred_element_type=F32)
    u = jnp.square(jnp.maximum(a, 0.0)).astype(BF16)
    o_ref[...] += jnp.dot(u, w2_ref[...], preferred_element_type=F32)

    if final_norm:
        @pl.when(j == pl.num_programs(1) - 1)
        def _():
            o_ref[...] = _rmsnorm(o_ref[...], gf_ref[...])


def _mlp(x, g, w1, w2, g_final, final_norm):
    s, d = x.shape
    ff = w1.shape[1]
    tm, tf = MLP_TM, MLP_TF
    assert s % tm == 0 and ff % tf == 0
    vmem = (4 * tm * d * 4 + tm * d * 2 + 4 * d * tf * 2
            + tm * tf * 6 + 2 * tm * d * 4 + 4 * MIB)
    return pl.pallas_call(
        functools.partial(_mlp_kernel, final_norm=final_norm),
        grid=(s // tm, ff // tf),
        in_specs=[pl.BlockSpec((tm, d), lambda i, j: (i, 0)),
                  pl.BlockSpec((1, d), lambda i, j: (0, 0)),
                  pl.BlockSpec((d, tf), lambda i, j: (0, j)),
                  pl.BlockSpec((tf, d), lambda i, j: (j, 0)),
                  pl.BlockSpec((1, d), lambda i, j: (0, 0))],
        out_specs=pl.BlockSpec((tm, d), lambda i, j: (i, 0)),
        out_shape=jax.ShapeDtypeStruct((s, d), F32),
        scratch_shapes=[pltpu.VMEM((tm, d), BF16)],
        compiler_params=_params(("arbitrary", "arbitrary"), vmem),
        name="mlp",
    )(x, g, w1, w2, g_final)


def kernel(x, g_mix, w_in, b_f, b_gate, conv_w, w_conv_out, w_attn_out, w_mix_out, g_mlp, w_ff1, w_ff2, g_final):
    b, s, d = x.shape
    depth = g_mix.shape[0]
    c = w_conv_out.shape[1]
    a = w_attn_out.shape[1]
    n_heads = b_f.shape[1]
    assert a == n_heads * HEAD_DIM and w_in.shape[2] == 3 * c + 3 * a + n_heads + 2 * d
    o_q, o_k, o_v, o_f, o_g = 3 * c, 3 * c + a, 3 * c + 2 * a, 3 * c + 3 * a, 3 * c + 3 * a + n_heads

    w_in_b = lax.optimization_barrier(w_in.astype(BF16))

    outs = []
    for bi in range(b):
        xs = x[bi]
        for l in range(depth):
            w = w_in_b[l]
            w_cg = jnp.concatenate([w[:, o_g:], w[:, :o_q]], axis=1)
            wqvt = jnp.concatenate([w[:, o_q:o_k], w[:, o_v:o_f]], axis=1).T
            wf = jnp.pad(w[:, o_f:o_g], ((0, 0), (0, V7X_LANES - n_heads)))
            wkf = jnp.concatenate([w[:, o_k:o_v], wf], axis=1)
            bf = jnp.pad(b_f[l], (0, V7X_LANES - n_heads)).reshape(1, V7X_LANES)
            g1 = g_mix[l].reshape(1, d)

            z = _inproj(xs, g1, w_cg)
            qt, k, vt, r = _attn_prep(xs, g1, wqvt, wkf, bf)
            r_heads = r[:, 0, :n_heads].T
            attn = _flash(r_heads, qt, k, vt)
            xs = _mix(xs, z, attn, conv_w[l], b_gate[l].reshape(1, 2 * d),
                      w_conv_out[l].astype(BF16), w_attn_out[l].astype(BF16), w_mix_out[l].astype(BF16))
            xs = _mlp(xs, g_mlp[l].reshape(1, d), w_ff1[l].astype(BF16), w_ff2[l].astype(BF16),
                      g_final.reshape(1, d), final_norm=(l == depth - 1))
        outs.append(xs)
    return jnp.stack(outs, axis=0)
```

```python
import functools

import jax
import jax.numpy as jnp
from jax import lax
from jax.experimental import pallas as pl
from jax.experimental.pallas import tpu as pltpu

EPS = 1e-6
HEAD_DIM = 128
CONV_K = 3

V7X_LANES = 128
V7X_MXU_DEPTH = 256
V7X_F32_SUBLANES = 8
V7X_BF16_SUBLANES = 16
V7X_VMEM_BYTES = 64 * 1024 * 1024
MIB = 1024 * 1024

INPROJ_TM = 1024
INPROJ_TN = 1792
KV_BLOCK = 512
Q_BLOCK = 4 * KV_BLOCK
MIX_TM = 256
MLP_TM = 512
MLP_TF = 1024
SUM_ROWS = V7X_BF16_SUBLANES
V_ROWS = HEAD_DIM + SUM_ROWS

MASK_VALUE = -1e30
LOG2_E = 1.4426950408889634

BF16 = jnp.bfloat16
F32 = jnp.float32


def _rmsnorm(x, g):
    ms = jnp.mean(x * x, axis=-1, keepdims=True)
    return x * lax.rsqrt(ms + EPS) * g


def _split3_bf16(v):
    hi = v.astype(BF16).astype(F32)
    r1 = v - hi
    mid = r1.astype(BF16).astype(F32)
    lo = (r1 - mid).astype(BF16).astype(F32)
    return hi, mid, lo


def _params(semantics, vmem_bytes):
    assert vmem_bytes <= V7X_VMEM_BYTES - 4 * MIB, vmem_bytes
    return pltpu.CompilerParams(dimension_semantics=semantics, vmem_limit_bytes=int(vmem_bytes))


def _resident(block_shape, index_map):
    return pl.BlockSpec(block_shape, index_map, pipeline_mode=pl.Buffered(1))


def _inproj_kernel(x_ref, g_ref, w_ref, z_ref, h_ref):
    @pl.when(pl.program_id(1) == 0)
    def _():
        h_ref[...] = _rmsnorm(x_ref[...], g_ref[...]).astype(BF16)

    z_ref[...] = jnp.dot(h_ref[...], w_ref[...], preferred_element_type=F32).astype(z_ref.dtype)


def _inproj(x, g, w):
    s, d = x.shape
    n = w.shape[1]
    tm, tn = min(INPROJ_TM, s), INPROJ_TN
    assert s % tm == 0 and n % tn == 0
    vmem = (2 * tm * d * 4 + tm * d * 2 + 2 * d * tn * 2 + 2 * tm * tn * 2
            + tm * tn * 4 + tm * d * 4 + 4 * MIB)
    return pl.pallas_call(
        _inproj_kernel,
        grid=(s // tm, n // tn),
        in_specs=[pl.BlockSpec((tm, d), lambda i, j: (i, 0)),
                  pl.BlockSpec((1, d), lambda i, j: (0, 0)),
                  pl.BlockSpec((d, tn), lambda i, j: (0, j))],
        out_specs=pl.BlockSpec((tm, tn), lambda i, j: (i, j)),
        out_shape=jax.ShapeDtypeStruct((s, n), BF16),
        scratch_shapes=[pltpu.VMEM((tm, d), BF16)],
        compiler_params=_params(("arbitrary", "arbitrary"), vmem),
        name="inproj",
    )(x, g, w)


def _prep_kernel(x_ref, g_ref, wqvt_ref, wkf_ref, bf_ref,
                 qt_ref, k_ref, vt_ref, r_ref, carry_ref, *, n_heads, scale):
    i = pl.program_id(0)
    tm = x_ref.shape[0]

    @pl.when(i == 0)
    def _():
        carry_ref[...] = jnp.zeros_like(carry_ref)

    h = _rmsnorm(x_ref[...], g_ref[...]).astype(BF16)
    a = n_heads * HEAD_DIM
    qvt = lax.dot_general(wqvt_ref[...], h, (((1,), (1,)), ((), ())), preferred_element_type=F32)
    kf = jnp.dot(h, wkf_ref[...], preferred_element_type=F32)
    qt = qvt[0:a, :] * scale
    vt = qvt[a:, :]
    k = kf[:, 0:a]
    f = kf[:, a:] + bf_ref[...]

    lane = lax.broadcasted_iota(jnp.int32, f.shape, 1)
    logf = jnp.where(lane < n_heads, jax.nn.log_sigmoid(f) * LOG2_E, 0.0)

    row = lax.broadcasted_iota(jnp.int32, (tm, tm), 0)
    col = lax.broadcasted_iota(jnp.int32, (tm, tm), 1)
    tril = (col <= row).astype(BF16)
    parts = jnp.concatenate(_split3_bf16(logf), axis=1).astype(BF16)
    sums = jnp.dot(tril, parts, preferred_element_type=F32)
    c_rel = sums[:, 0:V7X_LANES] + sums[:, V7X_LANES:2 * V7X_LANES] + sums[:, 2 * V7X_LANES:]

    carry = carry_ref[...]
    r_ref[0] = carry
    carry_ref[...] = carry + c_rel[tm - 1:tm, :]

    nhi, nmid, nlo = _split3_bf16(-c_rel)
    bias_cols = (nhi + pltpu.roll(nmid, n_heads, 1) + pltpu.roll(nlo, 2 * n_heads, 1)).astype(BF16)

    sel_row = lax.broadcasted_iota(jnp.int32, (V7X_LANES, tm), 0)
    sum_rows = (lax.broadcasted_iota(jnp.int32, (SUM_ROWS, tm), 0) == 0).astype(BF16)
    for hh in range(n_heads):
        sl = slice(hh * HEAD_DIM, (hh + 1) * HEAD_DIM)
        onehot = ((sel_row == hh) | (sel_row == n_heads + hh) | (sel_row == 2 * n_heads + hh))
        qt_ref[hh, 0:HEAD_DIM, :] = qt[sl, :].astype(BF16)
        qt_ref[hh, HEAD_DIM:, :] = onehot.astype(BF16)
        k_ref[hh, 0, :, 0:HEAD_DIM] = k[:, sl].astype(BF16)
        k_ref[hh, 0, :, HEAD_DIM:] = bias_cols
        vt_ref[hh, 0, 0:HEAD_DIM, :] = vt[sl, :].astype(BF16)
        vt_ref[hh, 0, HEAD_DIM:, :] = sum_rows


def _attn_prep(x, g, wqvt, wkf, bf):
    s, d = x.shape
    a = wqvt.shape[0] // 2
    assert wkf.shape == (d, a + V7X_LANES)
    n_heads = a // HEAD_DIM
    assert 3 * n_heads <= V7X_LANES and 2 * HEAD_DIM == V7X_MXU_DEPTH
    tm = KV_BLOCK
    assert s % tm == 0
    nb = s // tm
    vmem = (2 * tm * d * 4 + 3 * d * a * 2 + d * V7X_LANES * 2
            + 2 * n_heads * tm * (2 * V7X_MXU_DEPTH + V_ROWS) * 2
            + tm * d * 6 + 3 * tm * a * 4 + tm * tm * 2 + 6 * MIB)
    kern = functools.partial(_prep_kernel, n_heads=n_heads, scale=LOG2_E * float(HEAD_DIM) ** -0.5)
    return pl.pallas_call(
        kern,
        grid=(nb,),
        in_specs=[pl.BlockSpec((tm, d), lambda i: (i, 0)),
                  _resident((1, d), lambda i: (0, 0)),
                  _resident((2 * a, d), lambda i: (0, 0)),
                  _resident((d, a + V7X_LANES), lambda i: (0, 0)),
                  _resident((1, V7X_LANES), lambda i: (0, 0))],
        out_specs=[pl.BlockSpec((n_heads, V7X_MXU_DEPTH, tm), lambda i: (0, 0, i)),
                   pl.BlockSpec((n_heads, 1, tm, V7X_MXU_DEPTH), lambda i: (0, i, 0, 0)),
                   pl.BlockSpec((n_heads, 1, V_ROWS, tm), lambda i: (0, i, 0, 0)),
                   pl.BlockSpec((1, V7X_F32_SUBLANES, V7X_LANES), lambda i: (i, 0, 0))],
        out_shape=[jax.ShapeDtypeStruct((n_heads, V7X_MXU_DEPTH, s), BF16),
                   jax.ShapeDtypeStruct((n_heads, nb, tm, V7X_MXU_DEPTH), BF16),
                   jax.ShapeDtypeStruct((n_heads, nb, V_ROWS, tm), BF16),
                   jax.ShapeDtypeStruct((nb, V7X_F32_SUBLANES, V7X_LANES), F32)],
        scratch_shapes=[pltpu.VMEM((V7X_F32_SUBLANES, V7X_LANES), F32)],
        compiler_params=_params(("arbitrary",), vmem),
        name="attn_prep",
    )(x, g, wqvt, wkf, bf)


def _flash_kernel(r_ref, qt_ref, k_ref, vt_ref, o_ref, s_ref, acc_ref):
    hd = pl.program_id(0)
    qi = pl.program_id(1)
    tk = k_ref.shape[1]
    tq = qt_ref.shape[1]

    sw = V7X_MXU_DEPTH
    strips = [slice(c * sw, (c + 1) * sw) for c in range(tq // sw)]

    def scores(blk, c):
        s = jnp.dot(k_ref[blk], qt_ref[:, strips[c]], preferred_element_type=F32)
        s_ref[:, strips[c]] = s
        return jnp.max(s, axis=0, keepdims=True)

    def attend(blk, c, m, bm, key0=None):
        s = s_ref[:, strips[c]]
        if key0 is not None:
            key = key0 + lax.broadcasted_iota(jnp.int32, (tk, sw), 0)
            qry = c * sw + lax.broadcasted_iota(jnp.int32, (tk, sw), 1)
            s = jnp.where(key <= qry, s, MASK_VALUE)
            bm = jnp.max(s, axis=0, keepdims=True)
        rb = r_ref[hd, blk]
        m_new = jnp.maximum(m, bm - rb)
        alpha = jnp.exp2(m - m_new)
        p = jnp.exp2(s - (m_new + rb)).astype(BF16)
        acc_ref[:, strips[c]] = alpha * acc_ref[:, strips[c]] + jnp.dot(
            vt_ref[blk], p, preferred_element_type=F32)
        return m_new

    def visible(key0, c):
        if key0 is None or key0 + tk - 1 <= c * sw:
            return "all"
        return "none" if key0 > (c + 1) * sw - 1 else "some"

    def block(t, carry, key0=None, next_key0=None, last=False):
        out = []
        for c in range(len(strips)):
            m, bm = carry[c]
            if visible(key0, c) != "none":
                m = attend(t, c, m, bm, key0 if visible(key0, c) == "some" else None)
            if not last and visible(next_key0, c) != "none":
                bm = scores(t + 1, c)
            out.append((m, bm))
        return tuple(out)

    acc_ref[...] = jnp.zeros_like(acc_ref)
    init = tuple((jnp.full((1, sw), MASK_VALUE, F32), scores(0, c)) for c in range(len(strips)))

    def blocks(first, count, carry):
        for j in range(count):
            carry = block(first + j, carry)
        return carry

    per_tile = tq // tk
    n_full = per_tile * qi
    carry = lax.fori_loop(0, n_full // 8, lambda w, carry: blocks(8 * w, 8, carry), init)
    done = 8 * (n_full // 8)
    for size in (4, 2):
        if any((per_tile * q) % 8 & size for q in range(8)):
            take = (n_full - done) >= size
            carry = lax.cond(take, functools.partial(blocks, done, size), lambda carry: carry, carry)
            done = done + jnp.where(take, size, 0)

    for j in range(per_tile):
        last = j == per_tile - 1
        carry = block(n_full + j, carry, key0=j * tk, next_key0=None if last else (j + 1) * tk, last=last)
    inv_l = 1.0 / acc_ref[HEAD_DIM:HEAD_DIM + 1, :]
    o_ref[...] = (acc_ref[0:HEAD_DIM, :] * inv_l).T.astype(o_ref.dtype)


def _flash(r, qt, k, vt):
    n_heads, depth, s = qt.shape
    nb, tk = k.shape[1], k.shape[2]
    tq = Q_BLOCK
    assert tq % (2 * tk) == 0 and s % tq == 0
    v_rows = vt.shape[2]
    vmem = (2 * nb * tk * (depth + v_rows) * 2 + 2 * depth * tq * 2 + 2 * tq * HEAD_DIM * 2
            + 2 * tk * tq * 6 + v_rows * tq * 4 + 3 * tk * tq * 4 + 4 * MIB)
    return pl.pallas_call(
        _flash_kernel,
        grid=(n_heads, s // tq),
        in_specs=[pl.BlockSpec(memory_space=pltpu.SMEM),
                  pl.BlockSpec((None, depth, tq), lambda h, i: (h, 0, i)),
                  pl.BlockSpec((None, nb, tk, depth), lambda h, i: (h, 0, 0, 0)),
                  pl.BlockSpec((None, nb, v_rows, tk), lambda h, i: (h, 0, 0, 0))],
        out_specs=pl.BlockSpec((tq, HEAD_DIM), lambda h, i: (i, h)),
        out_shape=jax.ShapeDtypeStruct((s, n_heads * HEAD_DIM), BF16),
        scratch_shapes=[pltpu.VMEM((tk, tq), F32),
                        pltpu.VMEM((v_rows, tq), F32)],
        compiler_params=_params(("arbitrary", "arbitrary"), vmem),
        name="flash",
    )(r, qt, k, vt)


def _mix_kernel(x_ref, gc_ref, ga_ref, cb_ref, cc_ref, cv_ref, ccp_ref, cvp_ref, at_ref,
                cw_ref, bg_ref, wco_ref, wao_ref, wmo_ref, o_ref, ext_ref):
    i = pl.program_id(0)
    tm = x_ref.shape[0]
    halo = ccp_ref.shape[0]
    d = x_ref.shape[1]

    prev = ccp_ref[...].astype(F32) * cvp_ref[...].astype(F32)
    ext_ref[0:halo, :] = jnp.where(i == 0, 0.0, prev)
    ext_ref[halo:, :] = cc_ref[...].astype(F32) * cv_ref[...].astype(F32)
    y = None
    for tap in range(CONV_K):
        shifted = ext_ref[pl.ds(halo - (CONV_K - 1) + tap, tm), :]
        term = cw_ref[tap:tap + 1, :] * shifted
        y = term if y is None else y + term
    conv_y = (cb_ref[...].astype(F32) * y).astype(BF16)

    conv_branch = jnp.dot(conv_y, wco_ref[...], preferred_element_type=F32)
    attn_branch = jnp.dot(at_ref[...], wao_ref[...], preferred_element_type=F32)
    gate_c = jax.nn.sigmoid(gc_ref[...].astype(F32) + bg_ref[:, 0:d])
    gate_a = jax.nn.sigmoid(ga_ref[...].astype(F32) + bg_ref[:, d:])
    merged = (gate_c * conv_branch + gate_a * attn_branch).astype(BF16)
    o_ref[...] = x_ref[...] + jnp.dot(merged, wmo_ref[...], preferred_element_type=F32)


def _mix(x, z, attn, conv_w, b_gate, wco, wao, wmo):
    s, d = x.shape
    c = wco.shape[0]
    a = wao.shape[0]
    assert d == 2 * c and z.shape[1] == 2 * d + 3 * c
    tm = MIX_TM
    halo = V7X_BF16_SUBLANES
    assert s % tm == 0 and tm % halo == 0 and halo >= CONV_K - 1
    rows_per_tile = tm // halo
    prev_rows = lambda i: jnp.maximum(i * rows_per_tile - 1, 0)
    vmem = (4 * tm * d * 4 + 4 * tm * d * 2 + 6 * tm * c * 2 + 2 * tm * a * 2
            + (c + a + d) * d * 2 + (tm + halo) * c * 4
            + 4 * tm * d * 4 + 3 * tm * c * 4 + 4 * MIB)
    return pl.pallas_call(
        _mix_kernel,
        grid=(s // tm,),
        in_specs=[pl.BlockSpec((tm, d), lambda i: (i, 0)),
                  pl.BlockSpec((tm, d), lambda i: (i, 0)),
                  pl.BlockSpec((tm, d), lambda i: (i, 1)),
                  pl.BlockSpec((tm, c), lambda i: (i, 4)),
                  pl.BlockSpec((tm, c), lambda i: (i, 5)),
                  pl.BlockSpec((tm, c), lambda i: (i, 6)),
                  pl.BlockSpec((halo, c), lambda i: (prev_rows(i), 5)),
                  pl.BlockSpec((halo, c), lambda i: (prev_rows(i), 6)),
                  pl.BlockSpec((tm, a), lambda i: (i, 0)),
                  _resident((CONV_K, c), lambda i: (0, 0)),
                  _resident((1, 2 * d), lambda i: (0, 0)),
                  _resident((c, d), lambda i: (0, 0)),
                  _resident((a, d), lambda i: (0, 0)),
                  _resident((d, d), lambda i: (0, 0))],
        out_specs=pl.BlockSpec((tm, d), lambda i: (i, 0)),
        out_shape=jax.ShapeDtypeStruct((s, d), F32),
        scratch_shapes=[pltpu.VMEM((tm + halo, c), F32)],
        compiler_params=_params(("arbitrary",), vmem),
        name="mix",
    )(x, z, z, z, z, z, z, z, attn, conv_w, b_gate, wco, wao, wmo)


def _mlp_kernel(x_ref, g_ref, w1_ref, w2_ref, gf_ref, o_ref, h_ref, *, final_norm):
    j = pl.program_id(1)

    @pl.when(j == 0)
    def _():
        x = x_ref[...]
        h_ref[...] = _rmsnorm(x, g_ref[...]).astype(BF16)
        o_ref[...] = x

    a = jnp.dot(h_ref[...], w1_ref[...], preferred_element_type=F32)
    u = jnp.square(jnp.maximum(a, 0.0)).astype(BF16)
    o_ref[...] += jnp.dot(u, w2_ref[...], preferred_element_type=F32)

    if final_norm:
        @pl.when(j == pl.num_programs(1) - 1)
        def _():
            o_ref[...] = _rmsnorm(o_ref[...], gf_ref[...])


def _mlp(x, g, w1, w2, g_final, final_norm):
    s, d = x.shape
    ff = w1.shape[1]
    tm, tf = MLP_TM, MLP_TF
    assert s % tm == 0 and ff % tf == 0
    vmem = (4 * tm * d * 4 + tm * d * 2 + 4 * d * tf * 2
            + tm * tf * 6 + 2 * tm * d * 4 + 4 * MIB)
    return pl.pallas_call(
        functools.partial(_mlp_kernel, final_norm=final_norm),
        grid=(s // tm, ff // tf),
        in_specs=[pl.BlockSpec((tm, d), lambda i, j: (i, 0)),
                  pl.BlockSpec((1, d), lambda i, j: (0, 0)),
                  pl.BlockSpec((d, tf), lambda i, j: (0, j)),
                  pl.BlockSpec((tf, d), lambda i, j: (j, 0)),
                  pl.BlockSpec((1, d), lambda i, j: (0, 0))],
        out_specs=pl.BlockSpec((tm, d), lambda i, j: (i, 0)),
        out_shape=jax.ShapeDtypeStruct((s, d), F32),
        scratch_shapes=[pltpu.VMEM((tm, d), BF16)],
        compiler_params=_params(("arbitrary", "arbitrary"), vmem),
        name="mlp",
    )(x, g, w1, w2, g_final)


def kernel(x, g_mix, w_in, b_f, b_gate, conv_w, w_conv_out, w_attn_out, w_mix_out, g_mlp, w_ff1, w_ff2, g_final):
    b, s, d = x.shape
    depth = g_mix.shape[0]
    c = w_conv_out.shape[1]
    a = w_attn_out.shape[1]
    n_heads = b_f.shape[1]
    assert a == n_heads * HEAD_DIM and w_in.shape[2] == 3 * c + 3 * a + n_heads + 2 * d
    o_q, o_k, o_v, o_f, o_g = 3 * c, 3 * c + a, 3 * c + 2 * a, 3 * c + 3 * a, 3 * c + 3 * a + n_heads

    outs = []
    for bi in range(b):
        xs = x[bi]
        for l in range(depth):
            w = w_in[l]
            w_cg = jnp.concatenate([w[:, o_g:], w[:, :o_q]], axis=1).astype(BF16)
            wqvt = jnp.concatenate([w[:, o_q:o_k], w[:, o_v:o_f]], axis=1).T.astype(BF16)
            wf = jnp.pad(w[:, o_f:o_g], ((0, 0), (0, V7X_LANES - n_heads)))
            wkf = jnp.concatenate([w[:, o_k:o_v], wf], axis=1).astype(BF16)
            bf = jnp.pad(b_f[l], (0, V7X_LANES - n_heads)).reshape(1, V7X_LANES)
            g1 = g_mix[l].reshape(1, d)

            z = _inproj(xs, g1, w_cg)
            qt, k, vt, r = _attn_prep(xs, g1, wqvt, wkf, bf)
            r_heads = r[:, 0, :n_heads].T
            attn = _flash(r_heads, qt, k, vt)
            xs = _mix(xs, z, attn, conv_w[l], b_gate[l].reshape(1, 2 * d),
                      w_conv_out[l].astype(BF16), w_attn_out[l].astype(BF16), w_mix_out[l].astype(BF16))
            xs = _mlp(xs, g_mlp[l].reshape(1, d), w_ff1[l].astype(BF16), w_ff2[l].astype(BF16),
                      g_final.reshape(1, d), final_norm=(l == depth - 1))
        outs.append(xs)
    return jnp.stack(outs, axis=0)
```

```python
import functools

import jax
import jax.numpy as jnp
from jax import lax
from jax.experimental import pallas as pl
from jax.experimental.pallas import tpu as pltpu

EPS = 1e-6
HEAD_DIM = 128
CONV_K = 3

V7X_LANES = 128
V7X_MXU_DEPTH = 256
V7X_F32_SUBLANES = 8
V7X_BF16_SUBLANES = 16
V7X_VMEM_BYTES = 64 * 1024 * 1024
MIB = 1024 * 1024

INPROJ_TM = 1024
INPROJ_TN = 1792
KV_BLOCK = 512
Q_BLOCK = 8 * KV_BLOCK
MIX_TM = 256
MLP_TM = 512
MLP_TF = 1024
SUM_ROWS = V7X_BF16_SUBLANES
V_ROWS = HEAD_DIM + SUM_ROWS

MASK_VALUE = -1e30
LOG2_E = 1.4426950408889634

BF16 = jnp.bfloat16
F32 = jnp.float32


def _rmsnorm(x, g):
    ms = jnp.mean(x * x, axis=-1, keepdims=True)
    return x * lax.rsqrt(ms + EPS) * g


def _split3_bf16(v):
    hi = v.astype(BF16).astype(F32)
    r1 = v - hi
    mid = r1.astype(BF16).astype(F32)
    lo = (r1 - mid).astype(BF16).astype(F32)
    return hi, mid, lo


def _params(semantics, vmem_bytes):
    assert vmem_bytes <= V7X_VMEM_BYTES - 4 * MIB, vmem_bytes
    return pltpu.CompilerParams(dimension_semantics=semantics, vmem_limit_bytes=int(vmem_bytes))


def _resident(block_shape, index_map):
    return pl.BlockSpec(block_shape, index_map, pipeline_mode=pl.Buffered(1))


def _inproj_kernel(x_ref, g_ref, w_ref, z_ref, h_ref):
    @pl.when(pl.program_id(1) == 0)
    def _():
        h_ref[...] = _rmsnorm(x_ref[...], g_ref[...]).astype(BF16)

    z_ref[...] = jnp.dot(h_ref[...], w_ref[...], preferred_element_type=F32).astype(z_ref.dtype)


def _inproj(x, g, w):
    s, d = x.shape
    n = w.shape[1]
    tm, tn = min(INPROJ_TM, s), INPROJ_TN
    assert s % tm == 0 and n % tn == 0
    vmem = (2 * tm * d * 4 + tm * d * 2 + 2 * d * tn * 2 + 2 * tm * tn * 2
            + tm * tn * 4 + tm * d * 4 + 4 * MIB)
    return pl.pallas_call(
        _inproj_kernel,
        grid=(s // tm, n // tn),
        in_specs=[pl.BlockSpec((tm, d), lambda i, j: (i, 0)),
                  pl.BlockSpec((1, d), lambda i, j: (0, 0)),
                  pl.BlockSpec((d, tn), lambda i, j: (0, j))],
        out_specs=pl.BlockSpec((tm, tn), lambda i, j: (i, j)),
        out_shape=jax.ShapeDtypeStruct((s, n), BF16),
        scratch_shapes=[pltpu.VMEM((tm, d), BF16)],
        compiler_params=_params(("arbitrary", "arbitrary"), vmem),
        name="inproj",
    )(x, g, w)


def _prep_kernel(x_ref, g_ref, wqvt_ref, wkf_ref, bf_ref,
                 qt_ref, k_ref, vt_ref, r_ref, carry_ref, *, n_heads, scale):
    i = pl.program_id(0)
    tm = x_ref.shape[0]

    @pl.when(i == 0)
    def _():
        carry_ref[...] = jnp.zeros_like(carry_ref)

    h = _rmsnorm(x_ref[...], g_ref[...]).astype(BF16)
    a = n_heads * HEAD_DIM
    qvt = lax.dot_general(wqvt_ref[...], h, (((1,), (1,)), ((), ())), preferred_element_type=F32)
    kf = jnp.dot(h, wkf_ref[...], preferred_element_type=F32)
    qt = qvt[0:a, :] * scale
    vt = qvt[a:, :]
    k = kf[:, 0:a]
    f = kf[:, a:] + bf_ref[...]

    lane = lax.broadcasted_iota(jnp.int32, f.shape, 1)
    logf = jnp.where(lane < n_heads, jax.nn.log_sigmoid(f) * LOG2_E, 0.0)

    row = lax.broadcasted_iota(jnp.int32, (tm, tm), 0)
    col = lax.broadcasted_iota(jnp.int32, (tm, tm), 1)
    tril = (col <= row).astype(BF16)
    parts = jnp.concatenate(_split3_bf16(logf), axis=1).astype(BF16)
    sums = jnp.dot(tril, parts, preferred_element_type=F32)
    c_rel = sums[:, 0:V7X_LANES] + sums[:, V7X_LANES:2 * V7X_LANES] + sums[:, 2 * V7X_LANES:]

    carry = carry_ref[...]
    r_ref[0] = carry
    carry_ref[...] = carry + c_rel[tm - 1:tm, :]

    nhi, nmid, nlo = _split3_bf16(-c_rel)
    bias_cols = (nhi + pltpu.roll(nmid, n_heads, 1) + pltpu.roll(nlo, 2 * n_heads, 1)).astype(BF16)

    sel_row = lax.broadcasted_iota(jnp.int32, (V7X_LANES, tm), 0)
    sum_rows = (lax.broadcasted_iota(jnp.int32, (SUM_ROWS, tm), 0) == 0).astype(BF16)
    for hh in range(n_heads):
        sl = slice(hh * HEAD_DIM, (hh + 1) * HEAD_DIM)
        onehot = ((sel_row == hh) | (sel_row == n_heads + hh) | (sel_row == 2 * n_heads + hh))
        qt_ref[hh, 0:HEAD_DIM, :] = qt[sl, :].astype(BF16)
        qt_ref[hh, HEAD_DIM:, :] = onehot.astype(BF16)
        k_ref[hh, 0, :, 0:HEAD_DIM] = k[:, sl].astype(BF16)
        k_ref[hh, 0, :, HEAD_DIM:] = bias_cols
        vt_ref[hh, 0, 0:HEAD_DIM, :] = vt[sl, :].astype(BF16)
        vt_ref[hh, 0, HEAD_DIM:, :] = sum_rows


def _attn_prep(x, g, wqvt, wkf, bf):
    s, d = x.shape
    a = wqvt.shape[0] // 2
    assert wkf.shape == (d, a + V7X_LANES)
    n_heads = a // HEAD_DIM
    assert 3 * n_heads <= V7X_LANES and 2 * HEAD_DIM == V7X_MXU_DEPTH
    tm = KV_BLOCK
    assert s % tm == 0
    nb = s // tm
    vmem = (2 * tm * d * 4 + 3 * d * a * 2 + d * V7X_LANES * 2
            + 2 * n_heads * tm * (2 * V7X_MXU_DEPTH + V_ROWS) * 2
            + tm * d * 6 + 3 * tm * a * 4 + tm * tm * 2 + 6 * MIB)
    kern = functools.partial(_prep_kernel, n_heads=n_heads, scale=LOG2_E * float(HEAD_DIM) ** -0.5)
    return pl.pallas_call(
        kern,
        grid=(nb,),
        in_specs=[pl.BlockSpec((tm, d), lambda i: (i, 0)),
                  _resident((1, d), lambda i: (0, 0)),
                  _resident((2 * a, d), lambda i: (0, 0)),
                  _resident((d, a + V7X_LANES), lambda i: (0, 0)),
                  _resident((1, V7X_LANES), lambda i: (0, 0))],
        out_specs=[pl.BlockSpec((n_heads, V7X_MXU_DEPTH, tm), lambda i: (0, 0, i)),
                   pl.BlockSpec((n_heads, 1, tm, V7X_MXU_DEPTH), lambda i: (0, i, 0, 0)),
                   pl.BlockSpec((n_heads, 1, V_ROWS, tm), lambda i: (0, i, 0, 0)),
                   pl.BlockSpec((1, V7X_F32_SUBLANES, V7X_LANES), lambda i: (i, 0, 0))],
        out_shape=[jax.ShapeDtypeStruct((n_heads, V7X_MXU_DEPTH, s), BF16),
                   jax.ShapeDtypeStruct((n_heads, nb, tm, V7X_MXU_DEPTH), BF16),
                   jax.ShapeDtypeStruct((n_heads, nb, V_ROWS, tm), BF16),
                   jax.ShapeDtypeStruct((nb, V7X_F32_SUBLANES, V7X_LANES), F32)],
        scratch_shapes=[pltpu.VMEM((V7X_F32_SUBLANES, V7X_LANES), F32)],
        compiler_params=_params(("arbitrary",), vmem),
        name="attn_prep",
    )(x, g, wqvt, wkf, bf)


def _flash_kernel(r_ref, qt_ref, k_ref, vt_ref, o_ref, s_ref, acc_ref):
    hd = pl.program_id(0)
    qi = pl.program_id(1)
    tk = k_ref.shape[1]
    tq = qt_ref.shape[1]

    sw = V7X_MXU_DEPTH
    strips = [slice(c * sw, (c + 1) * sw) for c in range(tq // sw)]

    def scores(blk, c):
        s = jnp.dot(k_ref[blk], qt_ref[:, strips[c]], preferred_element_type=F32)
        s_ref[:, strips[c]] = s
        return jnp.max(s, axis=0, keepdims=True)

    def attend(blk, c, m, bm, key0=None):
        s = s_ref[:, strips[c]]
        if key0 is not None:
            key = key0 + lax.broadcasted_iota(jnp.int32, (tk, sw), 0)
            qry = c * sw + lax.broadcasted_iota(jnp.int32, (tk, sw), 1)
            s = jnp.where(key <= qry, s, MASK_VALUE)
            bm = jnp.max(s, axis=0, keepdims=True)
        rb = r_ref[hd, blk]
        m_new = jnp.maximum(m, bm - rb)
        alpha = jnp.exp2(m - m_new)
        p = jnp.exp2(s - (m_new + rb)).astype(BF16)
        acc_ref[:, strips[c]] = alpha * acc_ref[:, strips[c]] + jnp.dot(
            vt_ref[blk], p, preferred_element_type=F32)
        return m_new

    def visible(key0, c):
        if key0 is None or key0 + tk - 1 <= c * sw:
            return "all"
        return "none" if key0 > (c + 1) * sw - 1 else "some"

    def block(t, carry, key0=None, next_key0=None, last=False):
        out = []
        for c in range(len(strips)):
            m, bm = carry[c]
            if visible(key0, c) != "none":
                m = attend(t, c, m, bm, key0 if visible(key0, c) == "some" else None)
            if not last and visible(next_key0, c) != "none":
                bm = scores(t + 1, c)
            out.append((m, bm))
        return tuple(out)

    acc_ref[...] = jnp.zeros_like(acc_ref)
    init = tuple((jnp.full((1, sw), MASK_VALUE, F32), scores(0, c)) for c in range(len(strips)))

    def blocks(first, count, carry):
        for j in range(count):
            carry = block(first + j, carry)
        return carry

    per_tile = tq // tk
    n_full = per_tile * qi
    carry = lax.fori_loop(0, n_full // 8, lambda w, carry: blocks(8 * w, 8, carry), init)
    done = 8 * (n_full // 8)
    for size in (4, 2):
        if any((per_tile * q) % 8 & size for q in range(8)):
            take = (n_full - done) >= size
            carry = lax.cond(take, functools.partial(blocks, done, size), lambda carry: carry, carry)
            done = done + jnp.where(take, size, 0)

    for j in range(per_tile):
        last = j == per_tile - 1
        carry = block(n_full + j, carry, key0=j * tk, next_key0=None if last else (j + 1) * tk, last=last)
    inv_l = 1.0 / acc_ref[HEAD_DIM:HEAD_DIM + 1, :]
    o_ref[...] = (acc_ref[0:HEAD_DIM, :] * inv_l).T.astype(o_ref.dtype)


def _flash(r, qt, k, vt):
    n_heads, depth, s = qt.shape
    nb, tk = k.shape[1], k.shape[2]
    tq = Q_BLOCK
    assert tq % (2 * tk) == 0 and s % tq == 0
    v_rows = vt.shape[2]
    vmem = (2 * nb * tk * (depth + v_rows) * 2 + 2 * depth * tq * 2 + 2 * tq * HEAD_DIM * 2
            + tk * tq * 4 + v_rows * tq * 4 + 8 * tk * V7X_MXU_DEPTH * 4 + 4 * MIB)
    return pl.pallas_call(
        _flash_kernel,
        grid=(n_heads, s // tq),
        in_specs=[pl.BlockSpec(memory_space=pltpu.SMEM),
                  pl.BlockSpec((None, depth, tq), lambda h, i: (h, 0, i)),
                  pl.BlockSpec((None, nb, tk, depth), lambda h, i: (h, 0, 0, 0)),
                  pl.BlockSpec((None, nb, v_rows, tk), lambda h, i: (h, 0, 0, 0))],
        out_specs=pl.BlockSpec((tq, HEAD_DIM), lambda h, i: (i, h)),
        out_shape=jax.ShapeDtypeStruct((s, n_heads * HEAD_DIM), BF16),
        scratch_shapes=[pltpu.VMEM((tk, tq), F32),
                        pltpu.VMEM((v_rows, tq), F32)],
        compiler_params=_params(("arbitrary", "arbitrary"), vmem),
        name="flash",
    )(r, qt, k, vt)


def _mix_kernel(x_ref, gc_ref, ga_ref, cb_ref, cc_ref, cv_ref, ccp_ref, cvp_ref, at_ref,
                cw_ref, bg_ref, wco_ref, wao_ref, wmo_ref, o_ref, ext_ref):
    i = pl.program_id(0)
    tm = x_ref.shape[0]
    halo = ccp_ref.shape[0]
    d = x_ref.shape[1]

    prev = ccp_ref[...].astype(F32) * cvp_ref[...].astype(F32)
    ext_ref[0:halo, :] = jnp.where(i == 0, 0.0, prev)
    ext_ref[halo:, :] = cc_ref[...].astype(F32) * cv_ref[...].astype(F32)
    y = None
    for tap in range(CONV_K):
        shifted = ext_ref[pl.ds(halo - (CONV_K - 1) + tap, tm), :]
        term = cw_ref[tap:tap + 1, :] * shifted
        y = term if y is None else y + term
    conv_y = (cb_ref[...].astype(F32) * y).astype(BF16)

    conv_branch = jnp.dot(conv_y, wco_ref[...], preferred_element_type=F32)
    attn_branch = jnp.dot(at_ref[...], wao_ref[...], preferred_element_type=F32)
    gate_c = jax.nn.sigmoid(gc_ref[...].astype(F32) + bg_ref[:, 0:d])
    gate_a = jax.nn.sigmoid(ga_ref[...].astype(F32) + bg_ref[:, d:])
    merged = (gate_c * conv_branch + gate_a * attn_branch).astype(BF16)
    o_ref[...] = x_ref[...] + jnp.dot(merged, wmo_ref[...], preferred_element_type=F32)


def _mix(x, z, attn, conv_w, b_gate, wco, wao, wmo):
    s, d = x.shape
    c = wco.shape[0]
    a = wao.shape[0]
    assert d == 2 * c and z.shape[1] == 2 * d + 3 * c
    tm = MIX_TM
    halo = V7X_BF16_SUBLANES
    assert s % tm == 0 and tm % halo == 0 and halo >= CONV_K - 1
    rows_per_tile = tm // halo
    prev_rows = lambda i: jnp.maximum(i * rows_per_tile - 1, 0)
    vmem = (4 * tm * d * 4 + 4 * tm * d * 2 + 6 * tm * c * 2 + 2 * tm * a * 2
            + (c + a + d) * d * 2 + (tm + halo) * c * 4
            + 4 * tm * d * 4 + 3 * tm * c * 4 + 4 * MIB)
    return pl.pallas_call(
        _mix_kernel,
        grid=(s // tm,),
        in_specs=[pl.BlockSpec((tm, d), lambda i: (i, 0)),
                  pl.BlockSpec((tm, d), lambda i: (i, 0)),
                  pl.BlockSpec((tm, d), lambda i: (i, 1)),
                  pl.BlockSpec((tm, c), lambda i: (i, 4)),
                  pl.BlockSpec((tm, c), lambda i: (i, 5)),
                  pl.BlockSpec((tm, c), lambda i: (i, 6)),
                  pl.BlockSpec((halo, c), lambda i: (prev_rows(i), 5)),
                  pl.BlockSpec((halo, c), lambda i: (prev_rows(i), 6)),
                  pl.BlockSpec((tm, a), lambda i: (i, 0)),
                  _resident((CONV_K, c), lambda i: (0, 0)),
                  _resident((1, 2 * d), lambda i: (0, 0)),
                  _resident((c, d), lambda i: (0, 0)),
                  _resident((a, d), lambda i: (0, 0)),
                  _resident((d, d), lambda i: (0, 0))],
        out_specs=pl.BlockSpec((tm, d), lambda i: (i, 0)),
        out_shape=jax.ShapeDtypeStruct((s, d), F32),
        scratch_shapes=[pltpu.VMEM((tm + halo, c), F32)],
        compiler_params=_params(("arbitrary",), vmem),
        name="mix",
    )(x, z, z, z, z, z, z, z, attn, conv_w, b_gate, wco, wao, wmo)


def _mlp_kernel(x_ref, g_ref, w1_ref, w2_ref, gf_ref, o_ref, h_ref, *, final_norm):
    j = pl.program_id(1)

    @pl.when(j == 0)
    def _():
        x = x_ref[...]
        h_ref[...] = _rmsnorm(x, g_ref[...]).astype(BF16)
        o_ref[...] = x

    a = jnp.dot(h_ref[...], w1_ref[...], preferred_element_type=F32)
    u = jnp.square(jnp.maximum(a, 0.0)).astype(BF16)
    o_ref[...] += jnp.dot(u, w2_ref[...], preferred_element_type=F32)

    if final_norm:
        @pl.when(j == pl.num_programs(1) - 1)
        def _():
            o_ref[...] = _rmsnorm(o_ref[...], gf_ref[...])


def _mlp(x, g, w1, w2, g_final, final_norm):
    s, d = x.shape
    ff = w1.shape[1]
    tm, tf = MLP_TM, MLP_TF
    assert s % tm == 0 and ff % tf == 0
    vmem = (4 * tm * d * 4 + tm * d * 2 + 4 * d * tf * 2
            + tm * tf * 6 + 2 * tm * d * 4 + 4 * MIB)
    return pl.pallas_call(
        functools.partial(_mlp_kernel, final_norm=final_norm),
        grid=(s // tm, ff // tf),
        in_specs=[pl.BlockSpec((tm, d), lambda i, j: (i, 0)),
                  pl.BlockSpec((1, d), lambda i, j: (0, 0)),
                  pl.BlockSpec((d, tf), lambda i, j: (0, j)),
                  pl.BlockSpec((tf, d), lambda i, j: (j, 0)),
                  pl.BlockSpec((1, d), lambda i, j: (0, 0))],
        out_specs=pl.BlockSpec((tm, d), lambda i, j: (i, 0)),
        out_shape=jax.ShapeDtypeStruct((s, d), F32),
        scratch_shapes=[pltpu.VMEM((tm, d), BF16)],
        compiler_params=_params(("arbitrary", "arbitrary"), vmem),
        name="mlp",
    )(x, g, w1, w2, g_final)


def kernel(x, g_mix, w_in, b_f, b_gate, conv_w, w_conv_out, w_attn_out, w_mix_out, g_mlp, w_ff1, w_ff2, g_final):
    b, s, d = x.shape
    depth = g_mix.shape[0]
    c = w_conv_out.shape[1]
    a = w_attn_out.shape[1]
    n_heads = b_f.shape[1]
    assert a == n_heads * HEAD_DIM and w_in.shape[2] == 3 * c + 3 * a + n_heads + 2 * d
    o_q, o_k, o_v, o_f, o_g = 3 * c, 3 * c + a, 3 * c + 2 * a, 3 * c + 3 * a, 3 * c + 3 * a + n_heads

    outs = []
    for bi in range(b):
        xs = x[bi]
        for l in range(depth):
            w = w_in[l]
            w_cg = jnp.concatenate([w[:, o_g:], w[:, :o_q]], axis=1).astype(BF16)
            wqvt = jnp.concatenate([w[:, o_q:o_k], w[:, o_v:o_f]], axis=1).T.astype(BF16)
            wf = jnp.pad(w[:, o_f:o_g], ((0, 0), (0, V7X_LANES - n_heads)))
            wkf = jnp.concatenate([w[:, o_k:o_v], wf], axis=1).astype(BF16)
            bf = jnp.pad(b_f[l], (0, V7X_LANES - n_heads)).reshape(1, V7X_LANES)
            g1 = g_mix[l].reshape(1, d)

            z = _inproj(xs, g1, w_cg)
            qt, k, vt, r = _attn_prep(xs, g1, wqvt, wkf, bf)
            r_heads = r[:, 0, :n_heads].T
            attn = _flash(r_heads, qt, k, vt)
            xs = _mix(xs, z, attn, conv_w[l], b_gate[l].reshape(1, 2 * d),
                      w_conv_out[l].astype(BF16), w_attn_out[l].astype(BF16), w_mix_out[l].astype(BF16))
            xs = _mlp(xs, g_mlp[l].reshape(1, d), w_ff1[l].astype(BF16), w_ff2[l].astype(BF16),
                      g_final.reshape(1, d), final_norm=(l == depth - 1))
        outs.append(xs)
    return jnp.stack(outs, axis=0)
```

```python
import functools

import jax
import jax.numpy as jnp
from jax import lax
from jax.experimental import pallas as pl
from jax.experimental.pallas import tpu as pltpu

EPS = 1e-6
HEAD_DIM = 128
CONV_K = 3

V7X_LANES = 128
V7X_MXU_DEPTH = 256
V7X_F32_SUBLANES = 8
V7X_BF16_SUBLANES = 16
V7X_VMEM_BYTES = 64 * 1024 * 1024
MIB = 1024 * 1024

INPROJ_TM = 1024
INPROJ_TN = 1792
KV_BLOCK = 512
Q_BLOCK = 8 * KV_BLOCK
MIX_TM = 256
MLP_TM = 512
MLP_TF = 2048
SUM_ROWS = V7X_BF16_SUBLANES
V_ROWS = HEAD_DIM + SUM_ROWS

MASK_VALUE = -1e30
LOG2_E = 1.4426950408889634

BF16 = jnp.bfloat16
F32 = jnp.float32


def _rmsnorm(x, g):
    ms = jnp.mean(x * x, axis=-1, keepdims=True)
    return x * lax.rsqrt(ms + EPS) * g


def _split3_bf16(v):
    hi = v.astype(BF16).astype(F32)
    r1 = v - hi
    mid = r1.astype(BF16).astype(F32)
    lo = (r1 - mid).astype(BF16).astype(F32)
    return hi, mid, lo


def _params(semantics, vmem_bytes):
    assert vmem_bytes <= V7X_VMEM_BYTES - 4 * MIB, vmem_bytes
    return pltpu.CompilerParams(dimension_semantics=semantics, vmem_limit_bytes=int(vmem_bytes))


def _resident(block_shape, index_map):
    return pl.BlockSpec(block_shape, index_map, pipeline_mode=pl.Buffered(1))


def _inproj_kernel(x_ref, g_ref, w_ref, z_ref, h_ref):
    @pl.when(pl.program_id(1) == 0)
    def _():
        h_ref[...] = _rmsnorm(x_ref[...], g_ref[...]).astype(BF16)

    z_ref[...] = jnp.dot(h_ref[...], w_ref[...], preferred_element_type=F32).astype(z_ref.dtype)


def _inproj(x, g, w):
    s, d = x.shape
    n = w.shape[1]
    tm, tn = min(INPROJ_TM, s), INPROJ_TN
    assert s % tm == 0 and n % tn == 0
    vmem = (2 * tm * d * 4 + tm * d * 2 + 2 * d * tn * 2 + 2 * tm * tn * 2
            + tm * tn * 4 + tm * d * 4 + 4 * MIB)
    return pl.pallas_call(
        _inproj_kernel,
        grid=(s // tm, n // tn),
        in_specs=[pl.BlockSpec((tm, d), lambda i, j: (i, 0)),
                  pl.BlockSpec((1, d), lambda i, j: (0, 0)),
                  pl.BlockSpec((d, tn), lambda i, j: (0, j))],
        out_specs=pl.BlockSpec((tm, tn), lambda i, j: (i, j)),
        out_shape=jax.ShapeDtypeStruct((s, n), BF16),
        scratch_shapes=[pltpu.VMEM((tm, d), BF16)],
        compiler_params=_params(("arbitrary", "arbitrary"), vmem),
        name="inproj",
    )(x, g, w)


def _prep_kernel(x_ref, g_ref, wqvt_ref, wkf_ref, bf_ref,
                 qt_ref, k_ref, vt_ref, r_ref, carry_ref, *, n_heads, scale):
    i = pl.program_id(0)
    tm = x_ref.shape[0]

    @pl.when(i == 0)
    def _():
        carry_ref[...] = jnp.zeros_like(carry_ref)

    h = _rmsnorm(x_ref[...], g_ref[...]).astype(BF16)
    a = n_heads * HEAD_DIM
    qvt = lax.dot_general(wqvt_ref[...], h, (((1,), (1,)), ((), ())), preferred_element_type=F32)
    kf = jnp.dot(h, wkf_ref[...], preferred_element_type=F32)
    qt = qvt[0:a, :] * scale
    vt = qvt[a:, :]
    k = kf[:, 0:a]
    f = kf[:, a:] + bf_ref[...]

    lane = lax.broadcasted_iota(jnp.int32, f.shape, 1)
    logf = jnp.where(lane < n_heads, jax.nn.log_sigmoid(f) * LOG2_E, 0.0)

    row = lax.broadcasted_iota(jnp.int32, (tm, tm), 0)
    col = lax.broadcasted_iota(jnp.int32, (tm, tm), 1)
    tril = (col <= row).astype(BF16)
    parts = jnp.concatenate(_split3_bf16(logf), axis=1).astype(BF16)
    sums = jnp.dot(tril, parts, preferred_element_type=F32)
    c_rel = sums[:, 0:V7X_LANES] + sums[:, V7X_LANES:2 * V7X_LANES] + sums[:, 2 * V7X_LANES:]

    carry = carry_ref[...]
    r_ref[0] = carry
    carry_ref[...] = carry + c_rel[tm - 1:tm, :]

    nhi, nmid, nlo = _split3_bf16(-c_rel)
    bias_cols = (nhi + pltpu.roll(nmid, n_heads, 1) + pltpu.roll(nlo, 2 * n_heads, 1)).astype(BF16)

    sel_row = lax.broadcasted_iota(jnp.int32, (V7X_LANES, tm), 0)
    sum_rows = (lax.broadcasted_iota(jnp.int32, (SUM_ROWS, tm), 0) == 0).astype(BF16)
    for hh in range(n_heads):
        sl = slice(hh * HEAD_DIM, (hh + 1) * HEAD_DIM)
        onehot = ((sel_row == hh) | (sel_row == n_heads + hh) | (sel_row == 2 * n_heads + hh))
        qt_ref[hh, 0:HEAD_DIM, :] = qt[sl, :].astype(BF16)
        qt_ref[hh, HEAD_DIM:, :] = onehot.astype(BF16)
        k_ref[hh, 0, :, 0:HEAD_DIM] = k[:, sl].astype(BF16)
        k_ref[hh, 0, :, HEAD_DIM:] = bias_cols
        vt_ref[hh, 0, 0:HEAD_DIM, :] = vt[sl, :].astype(BF16)
        vt_ref[hh, 0, HEAD_DIM:, :] = sum_rows


def _attn_prep(x, g, wqvt, wkf, bf):
    s, d = x.shape
    a = wqvt.shape[0] // 2
    assert wkf.shape == (d, a + V7X_LANES)
    n_heads = a // HEAD_DIM
    assert 3 * n_heads <= V7X_LANES and 2 * HEAD_DIM == V7X_MXU_DEPTH
    tm = KV_BLOCK
    assert s % tm == 0
    nb = s // tm
    vmem = (2 * tm * d * 4 + 3 * d * a * 2 + d * V7X_LANES * 2
            + 2 * n_heads * tm * (2 * V7X_MXU_DEPTH + V_ROWS) * 2
            + tm * d * 6 + 3 * tm * a * 4 + tm * tm * 2 + 6 * MIB)
    kern = functools.partial(_prep_kernel, n_heads=n_heads, scale=LOG2_E * float(HEAD_DIM) ** -0.5)
    return pl.pallas_call(
        kern,
        grid=(nb,),
        in_specs=[pl.BlockSpec((tm, d), lambda i: (i, 0)),
                  _resident((1, d), lambda i: (0, 0)),
                  _resident((2 * a, d), lambda i: (0, 0)),
                  _resident((d, a + V7X_LANES), lambda i: (0, 0)),
                  _resident((1, V7X_LANES), lambda i: (0, 0))],
        out_specs=[pl.BlockSpec((n_heads, V7X_MXU_DEPTH, tm), lambda i: (0, 0, i)),
                   pl.BlockSpec((n_heads, 1, tm, V7X_MXU_DEPTH), lambda i: (0, i, 0, 0)),
                   pl.BlockSpec((n_heads, 1, V_ROWS, tm), lambda i: (0, i, 0, 0)),
                   pl.BlockSpec((1, V7X_F32_SUBLANES, V7X_LANES), lambda i: (i, 0, 0))],
        out_shape=[jax.ShapeDtypeStruct((n_heads, V7X_MXU_DEPTH, s), BF16),
                   jax.ShapeDtypeStruct((n_heads, nb, tm, V7X_MXU_DEPTH), BF16),
                   jax.ShapeDtypeStruct((n_heads, nb, V_ROWS, tm), BF16),
                   jax.ShapeDtypeStruct((nb, V7X_F32_SUBLANES, V7X_LANES), F32)],
        scratch_shapes=[pltpu.VMEM((V7X_F32_SUBLANES, V7X_LANES), F32)],
        compiler_params=_params(("arbitrary",), vmem),
        name="attn_prep",
    )(x, g, wqvt, wkf, bf)


def _flash_kernel(r_ref, qt_ref, k_ref, vt_ref, o_ref, s_ref, acc_ref):
    hd = pl.program_id(0)
    qi = pl.program_id(1)
    tk = k_ref.shape[1]
    tq = qt_ref.shape[1]

    sw = V7X_MXU_DEPTH
    strips = [slice(c * sw, (c + 1) * sw) for c in range(tq // sw)]

    def scores(blk, c):
        s = jnp.dot(k_ref[blk], qt_ref[:, strips[c]], preferred_element_type=F32)
        s_ref[:, strips[c]] = s
        return jnp.max(s, axis=0, keepdims=True)

    def attend(blk, c, m, bm, key0=None):
        s = s_ref[:, strips[c]]
        if key0 is not None:
            key = key0 + lax.broadcasted_iota(jnp.int32, (tk, sw), 0)
            qry = c * sw + lax.broadcasted_iota(jnp.int32, (tk, sw), 1)
            s = jnp.where(key <= qry, s, MASK_VALUE)
            bm = jnp.max(s, axis=0, keepdims=True)
        rb = r_ref[hd, blk]
        m_new = jnp.maximum(m, bm - rb)
        alpha = jnp.exp2(m - m_new)
        p = jnp.exp2(s - (m_new + rb)).astype(BF16)
        acc_ref[:, strips[c]] = alpha * acc_ref[:, strips[c]] + jnp.dot(
            vt_ref[blk], p, preferred_element_type=F32)
        return m_new

    def visible(key0, c):
        if key0 is None or key0 + tk - 1 <= c * sw:
            return "all"
        return "none" if key0 > (c + 1) * sw - 1 else "some"

    def block(t, carry, key0=None, next_key0=None, last=False):
        out = []
        for c in range(len(strips)):
            m, bm = carry[c]
            if visible(key0, c) != "none":
                m = attend(t, c, m, bm, key0 if visible(key0, c) == "some" else None)
            if not last and visible(next_key0, c) != "none":
                bm = scores(t + 1, c)
            out.append((m, bm))
        return tuple(out)

    acc_ref[...] = jnp.zeros_like(acc_ref)
    init = tuple((jnp.full((1, sw), MASK_VALUE, F32), scores(0, c)) for c in range(len(strips)))

    def blocks(first, count, carry):
        for j in range(count):
            carry = block(first + j, carry)
        return carry

    per_tile = tq // tk
    n_full = per_tile * qi
    carry = lax.fori_loop(0, n_full // 8, lambda w, carry: blocks(8 * w, 8, carry), init)
    done = 8 * (n_full // 8)
    for size in (4, 2):
        if any((per_tile * q) % 8 & size for q in range(8)):
            take = (n_full - done) >= size
            carry = lax.cond(take, functools.partial(blocks, done, size), lambda carry: carry, carry)
            done = done + jnp.where(take, size, 0)

    for j in range(per_tile):
        last = j == per_tile - 1
        carry = block(n_full + j, carry, key0=j * tk, next_key0=None if last else (j + 1) * tk, last=last)
    inv_l = 1.0 / acc_ref[HEAD_DIM:HEAD_DIM + 1, :]
    o_ref[...] = (acc_ref[0:HEAD_DIM, :] * inv_l).T.astype(o_ref.dtype)


def _flash(r, qt, k, vt):
    n_heads, depth, s = qt.shape
    nb, tk = k.shape[1], k.shape[2]
    tq = Q_BLOCK
    assert tq % (2 * tk) == 0 and s % tq == 0
    v_rows = vt.shape[2]
    vmem = (2 * nb * tk * (depth + v_rows) * 2 + 2 * depth * tq * 2 + 2 * tq * HEAD_DIM * 2
            + tk * tq * 4 + v_rows * tq * 4 + 8 * tk * V7X_MXU_DEPTH * 4 + 4 * MIB)
    return pl.pallas_call(
        _flash_kernel,
        grid=(n_heads, s // tq),
        in_specs=[pl.BlockSpec(memory_space=pltpu.SMEM),
                  pl.BlockSpec((None, depth, tq), lambda h, i: (h, 0, i)),
                  pl.BlockSpec((None, nb, tk, depth), lambda h, i: (h, 0, 0, 0)),
                  pl.BlockSpec((None, nb, v_rows, tk), lambda h, i: (h, 0, 0, 0))],
        out_specs=pl.BlockSpec((tq, HEAD_DIM), lambda h, i: (i, h)),
        out_shape=jax.ShapeDtypeStruct((s, n_heads * HEAD_DIM), BF16),
        scratch_shapes=[pltpu.VMEM((tk, tq), F32),
                        pltpu.VMEM((v_rows, tq), F32)],
        compiler_params=_params(("arbitrary", "arbitrary"), vmem),
        name="flash",
    )(r, qt, k, vt)


def _mix_kernel(x_ref, gc_ref, ga_ref, cb_ref, cc_ref, cv_ref, ccp_ref, cvp_ref, at_ref,
                cw_ref, bg_ref, wco_ref, wao_ref, wmo_ref, o_ref, ext_ref):
    i = pl.program_id(0)
    tm = x_ref.shape[0]
    halo = ccp_ref.shape[0]
    d = x_ref.shape[1]

    prev = ccp_ref[...].astype(F32) * cvp_ref[...].astype(F32)
    ext_ref[0:halo, :] = jnp.where(i == 0, 0.0, prev)
    ext_ref[halo:, :] = cc_ref[...].astype(F32) * cv_ref[...].astype(F32)
    y = None
    for tap in range(CONV_K):
        shifted = ext_ref[pl.ds(halo - (CONV_K - 1) + tap, tm), :]
        term = cw_ref[tap:tap + 1, :] * shifted
        y = term if y is None else y + term
    conv_y = (cb_ref[...].astype(F32) * y).astype(BF16)

    conv_branch = jnp.dot(conv_y, wco_ref[...], preferred_element_type=F32)
    attn_branch = jnp.dot(at_ref[...], wao_ref[...], preferred_element_type=F32)
    gate_c = jax.nn.sigmoid(gc_ref[...].astype(F32) + bg_ref[:, 0:d])
    gate_a = jax.nn.sigmoid(ga_ref[...].astype(F32) + bg_ref[:, d:])
    merged = (gate_c * conv_branch + gate_a * attn_branch).astype(BF16)
    o_ref[...] = x_ref[...] + jnp.dot(merged, wmo_ref[...], preferred_element_type=F32)


def _mix(x, z, attn, conv_w, b_gate, wco, wao, wmo):
    s, d = x.shape
    c = wco.shape[0]
    a = wao.shape[0]
    assert d == 2 * c and z.shape[1] == 2 * d + 3 * c
    tm = MIX_TM
    halo = V7X_BF16_SUBLANES
    assert s % tm == 0 and tm % halo == 0 and halo >= CONV_K - 1
    rows_per_tile = tm // halo
    prev_rows = lambda i: jnp.maximum(i * rows_per_tile - 1, 0)
    vmem = (4 * tm * d * 4 + 4 * tm * d * 2 + 6 * tm * c * 2 + 2 * tm * a * 2
            + (c + a + d) * d * 2 + (tm + halo) * c * 4
            + 4 * tm * d * 4 + 3 * tm * c * 4 + 4 * MIB)
    return pl.pallas_call(
        _mix_kernel,
        grid=(s // tm,),
        in_specs=[pl.BlockSpec((tm, d), lambda i: (i, 0)),
                  pl.BlockSpec((tm, d), lambda i: (i, 0)),
                  pl.BlockSpec((tm, d), lambda i: (i, 1)),
                  pl.BlockSpec((tm, c), lambda i: (i, 4)),
                  pl.BlockSpec((tm, c), lambda i: (i, 5)),
                  pl.BlockSpec((tm, c), lambda i: (i, 6)),
                  pl.BlockSpec((halo, c), lambda i: (prev_rows(i), 5)),
                  pl.BlockSpec((halo, c), lambda i: (prev_rows(i), 6)),
                  pl.BlockSpec((tm, a), lambda i: (i, 0)),
                  _resident((CONV_K, c), lambda i: (0, 0)),
                  _resident((1, 2 * d), lambda i: (0, 0)),
                  _resident((c, d), lambda i: (0, 0)),
                  _resident((a, d), lambda i: (0, 0)),
                  _resident((d, d), lambda i: (0, 0))],
        out_specs=pl.BlockSpec((tm, d), lambda i: (i, 0)),
        out_shape=jax.ShapeDtypeStruct((s, d), F32),
        scratch_shapes=[pltpu.VMEM((tm + halo, c), F32)],
        compiler_params=_params(("arbitrary",), vmem),
        name="mix",
    )(x, z, z, z, z, z, z, z, attn, conv_w, b_gate, wco, wao, wmo)


def _mlp_kernel(x_ref, g_ref, w1_ref, w2_ref, gf_ref, o_ref, h_ref, *, final_norm):
    j = pl.program_id(1)

    @pl.when(j == 0)
    def _():
        x = x_ref[...]
        h_ref[...] = _rmsnorm(x, g_ref[...]).astype(BF16)
        o_ref[...] = x

    a = jnp.dot(h_ref[...], w1_ref[...], preferred_element_type=F32)
    u = jnp.square(jnp.maximum(a, 0.0)).astype(BF16)
    o_ref[...] += jnp.dot(u, w2_ref[...], preferred_element_type=F32)

    if final_norm:
        @pl.when(j == pl.num_programs(1) - 1)
        def _():
            o_ref[...] = _rmsnorm(o_ref[...], gf_ref[...])


def _mlp(x, g, w1, w2, g_final, final_norm):
    s, d = x.shape
    ff = w1.shape[1]
    tm, tf = MLP_TM, MLP_TF
    assert s % tm == 0 and ff % tf == 0
    vmem = (4 * tm * d * 4 + tm * d * 2 + 4 * d * tf * 2
            + tm * tf * 6 + 2 * MIB)
    return pl.pallas_call(
        functools.partial(_mlp_kernel, final_norm=final_norm),
        grid=(s // tm, ff // tf),
        in_specs=[pl.BlockSpec((tm, d), lambda i, j: (i, 0)),
                  pl.BlockSpec((1, d), lambda i, j: (0, 0)),
                  pl.BlockSpec((d, tf), lambda i, j: (0, j)),
                  pl.BlockSpec((tf, d), lambda i, j: (j, 0)),
                  pl.BlockSpec((1, d), lambda i, j: (0, 0))],
        out_specs=pl.BlockSpec((tm, d), lambda i, j: (i, 0)),
        out_shape=jax.ShapeDtypeStruct((s, d), F32),
        scratch_shapes=[pltpu.VMEM((tm, d), BF16)],
        compiler_params=_params(("arbitrary", "arbitrary"), vmem),
        name="mlp",
    )(x, g, w1, w2, g_final)


def kernel(x, g_mix, w_in, b_f, b_gate, conv_w, w_conv_out, w_attn_out, w_mix_out, g_mlp, w_ff1, w_ff2, g_final):
    b, s, d = x.shape
    depth = g_mix.shape[0]
    c = w_conv_out.shape[1]
    a = w_attn_out.shape[1]
    n_heads = b_f.shape[1]
    assert a == n_heads * HEAD_DIM and w_in.shape[2] == 3 * c + 3 * a + n_heads + 2 * d
    o_q, o_k, o_v, o_f, o_g = 3 * c, 3 * c + a, 3 * c + 2 * a, 3 * c + 3 * a, 3 * c + 3 * a + n_heads

    outs = []
    for bi in range(b):
        xs = x[bi]
        for l in range(depth):
            w = w_in[l]
            w_cg = jnp.concatenate([w[:, o_g:], w[:, :o_q]], axis=1).astype(BF16)
            wqvt = jnp.concatenate([w[:, o_q:o_k], w[:, o_v:o_f]], axis=1).T.astype(BF16)
            wf = jnp.pad(w[:, o_f:o_g], ((0, 0), (0, V7X_LANES - n_heads)))
            wkf = jnp.concatenate([w[:, o_k:o_v], wf], axis=1).astype(BF16)
            bf = jnp.pad(b_f[l], (0, V7X_LANES - n_heads)).reshape(1, V7X_LANES)
            g1 = g_mix[l].reshape(1, d)

            z = _inproj(xs, g1, w_cg)
            qt, k, vt, r = _attn_prep(xs, g1, wqvt, wkf, bf)
            r_heads = r[:, 0, :n_heads].T
            attn = _flash(r_heads, qt, k, vt)
            xs = _mix(xs, z, attn, conv_w[l], b_gate[l].reshape(1, 2 * d),
                      w_conv_out[l].astype(BF16), w_attn_out[l].astype(BF16), w_mix_out[l].astype(BF16))
            xs = _mlp(xs, g_mlp[l].reshape(1, d), w_ff1[l].astype(BF16), w_ff2[l].astype(BF16),
                      g_final.reshape(1, d), final_norm=(l == depth - 1))
        outs.append(xs)
    return jnp.stack(outs, axis=0)
```

```python
import functools

import jax
import jax.numpy as jnp
from jax import lax
from jax.experimental import pallas as pl
from jax.experimental.pallas import tpu as pltpu

EPS = 1e-6
HEAD_DIM = 128
CONV_K = 3

V7X_LANES = 128
V7X_MXU_DEPTH = 256
V7X_F32_SUBLANES = 8
V7X_BF16_SUBLANES = 16
V7X_VMEM_BYTES = 64 * 1024 * 1024
MIB = 1024 * 1024

INPROJ_TM = 1024
INPROJ_TN = 1792
KV_BLOCK = 512
Q_BLOCK = 8 * KV_BLOCK
MIX_TM = 512
MLP_TM = 512
MLP_TF = 2048
SUM_ROWS = V7X_BF16_SUBLANES
V_ROWS = HEAD_DIM + SUM_ROWS

MASK_VALUE = -1e30
LOG2_E = 1.4426950408889634

BF16 = jnp.bfloat16
F32 = jnp.float32


def _rmsnorm(x, g):
    ms = jnp.mean(x * x, axis=-1, keepdims=True)
    return x * lax.rsqrt(ms + EPS) * g


def _split3_bf16(v):
    hi = v.astype(BF16).astype(F32)
    r1 = v - hi
    mid = r1.astype(BF16).astype(F32)
    lo = (r1 - mid).astype(BF16).astype(F32)
    return hi, mid, lo


def _params(semantics, vmem_bytes):
    assert vmem_bytes <= V7X_VMEM_BYTES - 4 * MIB, vmem_bytes
    return pltpu.CompilerParams(dimension_semantics=semantics, vmem_limit_bytes=int(vmem_bytes))


def _resident(block_shape, index_map):
    return pl.BlockSpec(block_shape, index_map, pipeline_mode=pl.Buffered(1))


def _inproj_kernel(x_ref, g_ref, w_ref, z_ref, h_ref):
    @pl.when(pl.program_id(1) == 0)
    def _():
        h_ref[...] = _rmsnorm(x_ref[...], g_ref[...]).astype(BF16)

    z_ref[...] = jnp.dot(h_ref[...], w_ref[...], preferred_element_type=F32).astype(z_ref.dtype)


def _inproj(x, g, w):
    s, d = x.shape
    n = w.shape[1]
    tm, tn = min(INPROJ_TM, s), INPROJ_TN
    assert s % tm == 0 and n % tn == 0
    vmem = (2 * tm * d * 4 + tm * d * 2 + 2 * d * tn * 2 + 2 * tm * tn * 2
            + tm * tn * 4 + tm * d * 4 + 4 * MIB)
    return pl.pallas_call(
        _inproj_kernel,
        grid=(s // tm, n // tn),
        in_specs=[pl.BlockSpec((tm, d), lambda i, j: (i, 0)),
                  pl.BlockSpec((1, d), lambda i, j: (0, 0)),
                  pl.BlockSpec((d, tn), lambda i, j: (0, j))],
        out_specs=pl.BlockSpec((tm, tn), lambda i, j: (i, j)),
        out_shape=jax.ShapeDtypeStruct((s, n), BF16),
        scratch_shapes=[pltpu.VMEM((tm, d), BF16)],
        compiler_params=_params(("arbitrary", "arbitrary"), vmem),
        name="inproj",
    )(x, g, w)


def _prep_kernel(x_ref, g_ref, wqvt_ref, wkf_ref, bf_ref,
                 qt_ref, k_ref, vt_ref, r_ref, carry_ref, *, n_heads, scale):
    i = pl.program_id(0)
    tm = x_ref.shape[0]

    @pl.when(i == 0)
    def _():
        carry_ref[...] = jnp.zeros_like(carry_ref)

    h = _rmsnorm(x_ref[...], g_ref[...]).astype(BF16)
    a = n_heads * HEAD_DIM
    qvt = lax.dot_general(wqvt_ref[...], h, (((1,), (1,)), ((), ())), preferred_element_type=F32)
    kf = jnp.dot(h, wkf_ref[...], preferred_element_type=F32)
    qt = qvt[0:a, :] * scale
    vt = qvt[a:, :]
    k = kf[:, 0:a]
    f = kf[:, a:] + bf_ref[...]

    lane = lax.broadcasted_iota(jnp.int32, f.shape, 1)
    logf = jnp.where(lane < n_heads, jax.nn.log_sigmoid(f) * LOG2_E, 0.0)

    row = lax.broadcasted_iota(jnp.int32, (tm, tm), 0)
    col = lax.broadcasted_iota(jnp.int32, (tm, tm), 1)
    tril = (col <= row).astype(BF16)
    parts = jnp.concatenate(_split3_bf16(logf), axis=1).astype(BF16)
    sums = jnp.dot(tril, parts, preferred_element_type=F32)
    c_rel = sums[:, 0:V7X_LANES] + sums[:, V7X_LANES:2 * V7X_LANES] + sums[:, 2 * V7X_LANES:]

    carry = carry_ref[...]
    r_ref[0] = carry
    carry_ref[...] = carry + c_rel[tm - 1:tm, :]

    nhi, nmid, nlo = _split3_bf16(-c_rel)
    bias_cols = (nhi + pltpu.roll(nmid, n_heads, 1) + pltpu.roll(nlo, 2 * n_heads, 1)).astype(BF16)

    sel_row = lax.broadcasted_iota(jnp.int32, (V7X_LANES, tm), 0)
    sum_rows = (lax.broadcasted_iota(jnp.int32, (SUM_ROWS, tm), 0) == 0).astype(BF16)
    for hh in range(n_heads):
        sl = slice(hh * HEAD_DIM, (hh + 1) * HEAD_DIM)
        onehot = ((sel_row == hh) | (sel_row == n_heads + hh) | (sel_row == 2 * n_heads + hh))
        qt_ref[hh, 0:HEAD_DIM, :] = qt[sl, :].astype(BF16)
        qt_ref[hh, HEAD_DIM:, :] = onehot.astype(BF16)
        k_ref[hh, 0, :, 0:HEAD_DIM] = k[:, sl].astype(BF16)
        k_ref[hh, 0, :, HEAD_DIM:] = bias_cols
        vt_ref[hh, 0, 0:HEAD_DIM, :] = vt[sl, :].astype(BF16)
        vt_ref[hh, 0, HEAD_DIM:, :] = sum_rows


def _attn_prep(x, g, wqvt, wkf, bf):
    s, d = x.shape
    a = wqvt.shape[0] // 2
    assert wkf.shape == (d, a + V7X_LANES)
    n_heads = a // HEAD_DIM
    assert 3 * n_heads <= V7X_LANES and 2 * HEAD_DIM == V7X_MXU_DEPTH
    tm = KV_BLOCK
    assert s % tm == 0
    nb = s // tm
    vmem = (2 * tm * d * 4 + 3 * d * a * 2 + d * V7X_LANES * 2
            + 2 * n_heads * tm * (2 * V7X_MXU_DEPTH + V_ROWS) * 2
            + tm * d * 6 + 3 * tm * a * 4 + tm * tm * 2 + 6 * MIB)
    kern = functools.partial(_prep_kernel, n_heads=n_heads, scale=LOG2_E * float(HEAD_DIM) ** -0.5)
    return pl.pallas_call(
        kern,
        grid=(nb,),
        in_specs=[pl.BlockSpec((tm, d), lambda i: (i, 0)),
                  _resident((1, d), lambda i: (0, 0)),
                  _resident((2 * a, d), lambda i: (0, 0)),
                  _resident((d, a + V7X_LANES), lambda i: (0, 0)),
                  _resident((1, V7X_LANES), lambda i: (0, 0))],
        out_specs=[pl.BlockSpec((n_heads, V7X_MXU_DEPTH, tm), lambda i: (0, 0, i)),
                   pl.BlockSpec((n_heads, 1, tm, V7X_MXU_DEPTH), lambda i: (0, i, 0, 0)),
                   pl.BlockSpec((n_heads, 1, V_ROWS, tm), lambda i: (0, i, 0, 0)),
                   pl.BlockSpec((1, V7X_F32_SUBLANES, V7X_LANES), lambda i: (i, 0, 0))],
        out_shape=[jax.ShapeDtypeStruct((n_heads, V7X_MXU_DEPTH, s), BF16),
                   jax.ShapeDtypeStruct((n_heads, nb, tm, V7X_MXU_DEPTH), BF16),
                   jax.ShapeDtypeStruct((n_heads, nb, V_ROWS, tm), BF16),
                   jax.ShapeDtypeStruct((nb, V7X_F32_SUBLANES, V7X_LANES), F32)],
        scratch_shapes=[pltpu.VMEM((V7X_F32_SUBLANES, V7X_LANES), F32)],
        compiler_params=_params(("arbitrary",), vmem),
        name="attn_prep",
    )(x, g, wqvt, wkf, bf)


def _flash_kernel(r_ref, qt_ref, k_ref, vt_ref, o_ref, s_ref, acc_ref):
    hd = pl.program_id(0)
    qi = pl.program_id(1)
    tk = k_ref.shape[1]
    tq = qt_ref.shape[1]

    sw = V7X_MXU_DEPTH
    strips = [slice(c * sw, (c + 1) * sw) for c in range(tq // sw)]

    def scores(blk, c):
        s = jnp.dot(k_ref[blk], qt_ref[:, strips[c]], preferred_element_type=F32)
        s_ref[:, strips[c]] = s
        return jnp.max(s, axis=0, keepdims=True)

    def attend(blk, c, m, bm, key0=None):
        s = s_ref[:, strips[c]]
        if key0 is not None:
            key = key0 + lax.broadcasted_iota(jnp.int32, (tk, sw), 0)
            qry = c * sw + lax.broadcasted_iota(jnp.int32, (tk, sw), 1)
            s = jnp.where(key <= qry, s, MASK_VALUE)
            bm = jnp.max(s, axis=0, keepdims=True)
        rb = r_ref[hd, blk]
        m_new = jnp.maximum(m, bm - rb)
        alpha = jnp.exp2(m - m_new)
        p = jnp.exp2(s - (m_new + rb)).astype(BF16)
        acc_ref[:, strips[c]] = alpha * acc_ref[:, strips[c]] + jnp.dot(
            vt_ref[blk], p, preferred_element_type=F32)
        return m_new

    def visible(key0, c):
        if key0 is None or key0 + tk - 1 <= c * sw:
            return "all"
        return "none" if key0 > (c + 1) * sw - 1 else "some"

    def block(t, carry, key0=None, next_key0=None, last=False):
        out = []
        for c in range(len(strips)):
            m, bm = carry[c]
            if visible(key0, c) != "none":
                m = attend(t, c, m, bm, key0 if visible(key0, c) == "some" else None)
            if not last and visible(next_key0, c) != "none":
                bm = scores(t + 1, c)
            out.append((m, bm))
        return tuple(out)

    acc_ref[...] = jnp.zeros_like(acc_ref)
    init = tuple((jnp.full((1, sw), MASK_VALUE, F32), scores(0, c)) for c in range(len(strips)))

    def blocks(first, count, carry):
        for j in range(count):
            carry = block(first + j, carry)
        return carry

    per_tile = tq // tk
    n_full = per_tile * qi
    carry = lax.fori_loop(0, n_full // 8, lambda w, carry: blocks(8 * w, 8, carry), init)
    done = 8 * (n_full // 8)
    for size in (4, 2):
        if any((per_tile * q) % 8 & size for q in range(8)):
            take = (n_full - done) >= size
            carry = lax.cond(take, functools.partial(blocks, done, size), lambda carry: carry, carry)
            done = done + jnp.where(take, size, 0)

    for j in range(per_tile):
        last = j == per_tile - 1
        carry = block(n_full + j, carry, key0=j * tk, next_key0=None if last else (j + 1) * tk, last=last)
    inv_l = 1.0 / acc_ref[HEAD_DIM:HEAD_DIM + 1, :]
    o_ref[...] = (acc_ref[0:HEAD_DIM, :] * inv_l).T.astype(o_ref.dtype)


def _flash(r, qt, k, vt):
    n_heads, depth, s = qt.shape
    nb, tk = k.shape[1], k.shape[2]
    tq = Q_BLOCK
    assert tq % (2 * tk) == 0 and s % tq == 0
    v_rows = vt.shape[2]
    vmem = (2 * nb * tk * (depth + v_rows) * 2 + 2 * depth * tq * 2 + 2 * tq * HEAD_DIM * 2
            + tk * tq * 4 + v_rows * tq * 4 + 8 * tk * V7X_MXU_DEPTH * 4 + 4 * MIB)
    return pl.pallas_call(
        _flash_kernel,
        grid=(n_heads, s // tq),
        in_specs=[pl.BlockSpec(memory_space=pltpu.SMEM),
                  pl.BlockSpec((None, depth, tq), lambda h, i: (h, 0, i)),
                  pl.BlockSpec((None, nb, tk, depth), lambda h, i: (h, 0, 0, 0)),
                  pl.BlockSpec((None, nb, v_rows, tk), lambda h, i: (h, 0, 0, 0))],
        out_specs=pl.BlockSpec((tq, HEAD_DIM), lambda h, i: (i, h)),
        out_shape=jax.ShapeDtypeStruct((s, n_heads * HEAD_DIM), BF16),
        scratch_shapes=[pltpu.VMEM((tk, tq), F32),
                        pltpu.VMEM((v_rows, tq), F32)],
        compiler_params=_params(("arbitrary", "arbitrary"), vmem),
        name="flash",
    )(r, qt, k, vt)


def _mix_kernel(x_ref, gc_ref, ga_ref, cb_ref, cc_ref, cv_ref, ccp_ref, cvp_ref, at_ref,
                cw_ref, bg_ref, wco_ref, wao_ref, wmo_ref, o_ref, ext_ref):
    i = pl.program_id(0)
    tm = x_ref.shape[0]
    halo = ccp_ref.shape[0]
    d = x_ref.shape[1]

    prev = ccp_ref[...].astype(F32) * cvp_ref[...].astype(F32)
    ext_ref[0:halo, :] = jnp.where(i == 0, 0.0, prev)
    ext_ref[halo:, :] = cc_ref[...].astype(F32) * cv_ref[...].astype(F32)
    y = None
    for tap in range(CONV_K):
        shifted = ext_ref[pl.ds(halo - (CONV_K - 1) + tap, tm), :]
        term = cw_ref[tap:tap + 1, :] * shifted
        y = term if y is None else y + term
    conv_y = (cb_ref[...].astype(F32) * y).astype(BF16)

    conv_branch = jnp.dot(conv_y, wco_ref[...], preferred_element_type=F32)
    attn_branch = jnp.dot(at_ref[...], wao_ref[...], preferred_element_type=F32)
    gate_c = jax.nn.sigmoid(gc_ref[...].astype(F32) + bg_ref[:, 0:d])
    gate_a = jax.nn.sigmoid(ga_ref[...].astype(F32) + bg_ref[:, d:])
    merged = (gate_c * conv_branch + gate_a * attn_branch).astype(BF16)
    o_ref[...] = x_ref[...] + jnp.dot(merged, wmo_ref[...], preferred_element_type=F32)


def _mix(x, z, attn, conv_w, b_gate, wco, wao, wmo):
    s, d = x.shape
    c = wco.shape[0]
    a = wao.shape[0]
    assert d == 2 * c and z.shape[1] == 2 * d + 3 * c
    tm = MIX_TM
    halo = V7X_BF16_SUBLANES
    assert s % tm == 0 and tm % halo == 0 and halo >= CONV_K - 1
    rows_per_tile = tm // halo
    prev_rows = lambda i: jnp.maximum(i * rows_per_tile - 1, 0)
    vmem = (4 * tm * d * 4 + 4 * tm * d * 2 + 6 * tm * c * 2 + 2 * tm * a * 2
            + (c + a + d) * d * 2 + (tm + halo) * c * 4
            + tm * d * 4 + tm * c * 4 + 2 * MIB)
    return pl.pallas_call(
        _mix_kernel,
        grid=(s // tm,),
        in_specs=[pl.BlockSpec((tm, d), lambda i: (i, 0)),
                  pl.BlockSpec((tm, d), lambda i: (i, 0)),
                  pl.BlockSpec((tm, d), lambda i: (i, 1)),
                  pl.BlockSpec((tm, c), lambda i: (i, 4)),
                  pl.BlockSpec((tm, c), lambda i: (i, 5)),
                  pl.BlockSpec((tm, c), lambda i: (i, 6)),
                  pl.BlockSpec((halo, c), lambda i: (prev_rows(i), 5)),
                  pl.BlockSpec((halo, c), lambda i: (prev_rows(i), 6)),
                  pl.BlockSpec((tm, a), lambda i: (i, 0)),
                  _resident((CONV_K, c), lambda i: (0, 0)),
                  _resident((1, 2 * d), lambda i: (0, 0)),
                  _resident((c, d), lambda i: (0, 0)),
                  _resident((a, d), lambda i: (0, 0)),
                  _resident((d, d), lambda i: (0, 0))],
        out_specs=pl.BlockSpec((tm, d), lambda i: (i, 0)),
        out_shape=jax.ShapeDtypeStruct((s, d), F32),
        scratch_shapes=[pltpu.VMEM((tm + halo, c), F32)],
        compiler_params=_params(("arbitrary",), vmem),
        name="mix",
    )(x, z, z, z, z, z, z, z, attn, conv_w, b_gate, wco, wao, wmo)


def _mlp_kernel(x_ref, g_ref, w1_ref, w2_ref, gf_ref, o_ref, h_ref, *, final_norm):
    j = pl.program_id(1)

    @pl.when(j == 0)
    def _():
        x = x_ref[...]
        h_ref[...] = _rmsnorm(x, g_ref[...]).astype(BF16)
        o_ref[...] = x

    a = jnp.dot(h_ref[...], w1_ref[...], preferred_element_type=F32)
    u = jnp.square(jnp.maximum(a, 0.0)).astype(BF16)
    o_ref[...] += jnp.dot(u, w2_ref[...], preferred_element_type=F32)

    if final_norm:
        @pl.when(j == pl.num_programs(1) - 1)
        def _():
            o_ref[...] = _rmsnorm(o_ref[...], gf_ref[...])


def _mlp(x, g, w1, w2, g_final, final_norm):
    s, d = x.shape
    ff = w1.shape[1]
    tm, tf = MLP_TM, MLP_TF
    assert s % tm == 0 and ff % tf == 0
    vmem = (4 * tm * d * 4 + tm * d * 2 + 4 * d * tf * 2
            + tm * tf * 6 + 2 * MIB)
    return pl.pallas_call(
        functools.partial(_mlp_kernel, final_norm=final_norm),
        grid=(s // tm, ff // tf),
        in_specs=[pl.BlockSpec((tm, d), lambda i, j: (i, 0)),
                  pl.BlockSpec((1, d), lambda i, j: (0, 0)),
                  pl.BlockSpec((d, tf), lambda i, j: (0, j)),
                  pl.BlockSpec((tf, d), lambda i, j: (j, 0)),
                  pl.BlockSpec((1, d), lambda i, j: (0, 0))],
        out_specs=pl.BlockSpec((tm, d), lambda i, j: (i, 0)),
        out_shape=jax.ShapeDtypeStruct((s, d), F32),
        scratch_shapes=[pltpu.VMEM((tm, d), BF16)],
        compiler_params=_params(("arbitrary", "arbitrary"), vmem),
        name="mlp",
    )(x, g, w1, w2, g_final)


def kernel(x, g_mix, w_in, b_f, b_gate, conv_w, w_conv_out, w_attn_out, w_mix_out, g_mlp, w_ff1, w_ff2, g_final):
    b, s, d = x.shape
    depth = g_mix.shape[0]
    c = w_conv_out.shape[1]
    a = w_attn_out.shape[1]
    n_heads = b_f.shape[1]
    assert a == n_heads * HEAD_DIM and w_in.shape[2] == 3 * c + 3 * a + n_heads + 2 * d
    o_q, o_k, o_v, o_f, o_g = 3 * c, 3 * c + a, 3 * c + 2 * a, 3 * c + 3 * a, 3 * c + 3 * a + n_heads

    outs = []
    for bi in range(b):
        xs = x[bi]
        for l in range(depth):
            w = w_in[l]
            w_cg = jnp.concatenate([w[:, o_g:], w[:, :o_q]], axis=1).astype(BF16)
            wqvt = jnp.concatenate([w[:, o_q:o_k], w[:, o_v:o_f]], axis=1).T.astype(BF16)
            wf = jnp.pad(w[:, o_f:o_g], ((0, 0), (0, V7X_LANES - n_heads)))
            wkf = jnp.concatenate([w[:, o_k:o_v], wf], axis=1).astype(BF16)
            bf = jnp.pad(b_f[l], (0, V7X_LANES - n_heads)).reshape(1, V7X_LANES)
            g1 = g_mix[l].reshape(1, d)

            z = _inproj(xs, g1, w_cg)
            qt, k, vt, r = _attn_prep(xs, g1, wqvt, wkf, bf)
            r_heads = r[:, 0, :n_heads].T
            attn = _flash(r_heads, qt, k, vt)
            xs = _mix(xs, z, attn, conv_w[l], b_gate[l].reshape(1, 2 * d),
                      w_conv_out[l].astype(BF16), w_attn_out[l].astype(BF16), w_mix_out[l].astype(BF16))
            xs = _mlp(xs, g_mlp[l].reshape(1, d), w_ff1[l].astype(BF16), w_ff2[l].astype(BF16),
                      g_final.reshape(1, d), final_norm=(l == depth - 1))
        outs.append(xs)
    return jnp.stack(outs, axis=0)
```
